```python
import jax, jax.numpy as jnp
from jax import lax
import numpy as np

D_MODEL = 1024
BATCH = 8
SEQ = 4096
DEPTH = 1

CHUNK = 64
A_HEADS = 8
A_EXPAND = 128
A_FDIM = A_HEADS * A_EXPAND
A_IDIM = D_MODEL
A_HEAD_I = A_IDIM // A_HEADS
B_EXPAND = 2
B_INNER = B_EXPAND * D_MODEL
B_HEADDIM = 64
B_HEADS = B_INNER // B_HEADDIM
B_GROUPS = 4
B_HG = B_HEADS // B_GROUPS
B_STATE = 128
B_CONV = 4
B_CONV_DIM = B_INNER + 2 * B_GROUPS * B_STATE
D_FF = -(-8 * D_MODEL // (3 * 256)) * 256
ALPHA = (2.0 * DEPTH) ** 0.25
BETA = (8.0 * DEPTH) ** -0.25
LN_EPS = 1e-5
RMS_EPS = 1e-6
IN_SPLITS = (A_FDIM, A_FDIM, A_IDIM, A_IDIM, B_INNER, B_CONV_DIM, B_HEADS, D_MODEL, D_MODEL)
IN_DIM = sum(IN_SPLITS)

kernel_name = "hybrid_hgrn2_mamba2_deepnorm_adaln"

F32 = jnp.float32


def layer_norm(x, g=None, b=None):
    x32 = x.astype(F32)
    mu = jnp.mean(x32, axis=-1, keepdims=True)
    xc = x32 - mu
    y = xc * lax.rsqrt(jnp.mean(xc * xc, axis=-1, keepdims=True) + LN_EPS)
    if g is not None:
        y = y * g.astype(F32) + b.astype(F32)
    return y.astype(x.dtype)


def rms_norm(x, w=None):
    x32 = x.astype(F32)
    y = x32 * lax.rsqrt(jnp.mean(x32 * x32, axis=-1, keepdims=True) + RMS_EPS)
    if w is not None:
        y = y * w.astype(F32)
    return y


def to_chunks(t):
    b, s = t.shape[:2]
    return jnp.moveaxis(t.reshape(b, s // CHUNK, CHUNK, *t.shape[2:]), 1, 0)


def from_chunks(t):
    t = jnp.moveaxis(t, 0, 1)
    return t.reshape(t.shape[0], t.shape[1] * t.shape[2], *t.shape[3:])


def causal_dwconv(x, w, b):
    k, ch = w.shape
    y = lax.conv_general_dilated(x, w[:, None, :], window_strides=(1,), padding=[(k - 1, 0)],
                                 dimension_numbers=("NWC", "WIO", "NWC"), feature_group_count=ch)
    return y + b


def hgrn2_mixer(q, f_logit, i, g, lb, w_gnorm):
    bsz, s, _ = q.shape
    f = lb + (1.0 - lb) * jax.nn.sigmoid(f_logit.astype(F32))
    log_f = jnp.log(f)
    k = 1.0 - f
    qf = jax.nn.silu(q.astype(F32)) * (A_EXPAND ** -0.5)
    hk = (bsz, s, A_HEADS, A_EXPAND)
    qh, kh, gh = qf.reshape(hk), k.reshape(hk), log_f.reshape(hk)
    vh = i.astype(F32).reshape(bsz, s, A_HEADS, A_HEAD_I)
    mask = jnp.tril(jnp.ones((CHUNK, CHUNK), bool))[None, :, :, None, None]

    def step(state, inp):
        qc, kc, gc, vc = inp
        bc = jnp.cumsum(gc, axis=1)
        diff = bc[:, :, None] - bc[:, None, :]
        decay = jnp.exp(jnp.where(mask, diff, -jnp.inf))
        scores = jnp.einsum("bthk,bshk,btshk->bhts", qc, kc, decay)
        o = jnp.einsum("bhts,bshv->bthv", scores, vc)
        o = o + jnp.einsum("bthk,bhkv->bthv", qc * jnp.exp(bc), state)
        b_last = bc[:, -1]
        state = state * jnp.exp(b_last)[..., None] + jnp.einsum(
            "bshk,bshv->bhkv", kc * jnp.exp(b_last[:, None] - bc), vc)
        return state, o

    s0 = jnp.zeros((bsz, A_HEADS, A_EXPAND, A_HEAD_I), F32)
    _, o = lax.scan(step, s0, (to_chunks(qh), to_chunks(kh), to_chunks(gh), to_chunks(vh)))
    o = from_chunks(o)
    o = rms_norm(o, w_gnorm) * jax.nn.silu(g.astype(F32)).reshape(o.shape)
    return o.reshape(bsz, s, A_IDIM)


def mamba2_mixer(z, xbc, dt, conv_w, conv_b, dt_bias, a_log, d_skip, w_norm):
    bsz, s, _ = z.shape
    xbc = jax.nn.silu(causal_dwconv(xbc, conv_w, conv_b)).astype(F32)
    xs = xbc[..., :B_INNER].reshape(bsz, s, B_GROUPS, B_HG, B_HEADDIM)
    bm = xbc[..., B_INNER:B_INNER + B_GROUPS * B_STATE].reshape(bsz, s, B_GROUPS, B_STATE)
    cm = xbc[..., B_INNER + B_GROUPS * B_STATE:].reshape(bsz, s, B_GROUPS, B_STATE)
    delta = jax.nn.softplus(dt.astype(F32) + dt_bias.astype(F32)).reshape(bsz, s, B_GROUPS, B_HG)
    a = -jnp.exp(a_log.astype(F32)).reshape(B_GROUPS, B_HG) * delta
    xdt = xs * delta[..., None]
    mask = jnp.tril(jnp.ones((CHUNK, CHUNK), bool))[None, :, :, None, None]

    def step(state, inp):
        xc, ac, bc, cc = inp
        acum = jnp.cumsum(ac, axis=1)
        seg = acum[:, :, None] - acum[:, None, :]
        decay = jnp.exp(jnp.where(mask, seg, -jnp.inf))
        cb = jnp.einsum("btgn,bsgn->btsg", cc, bc)
        y = jnp.einsum("btsg,btsgh,bsghp->btghp", cb, decay, xc)
        y = y + jnp.einsum("btgn,bghpn->btghp", cc, state) * jnp.exp(acum)[..., None]
        a_last = acum[:, -1]
        state = state * jnp.exp(a_last)[..., None, None] + jnp.einsum(
            "bsgn,bsgh,bsghp->bghpn", bc, jnp.exp(a_last[:, None] - acum), xc)
        return state, y

    s0 = jnp.zeros((bsz, B_GROUPS, B_HG, B_HEADDIM, B_STATE), F32)
    _, y = lax.scan(step, s0, (to_chunks(xdt), to_chunks(a), to_chunks(bm), to_chunks(cm)))
    y = from_chunks(y) + xs * d_skip.astype(F32).reshape(B_GROUPS, B_HG)[..., None]
    y = y.reshape(bsz, s, B_INNER) * jax.nn.silu(z.astype(F32))
    y = rms_norm(y.reshape(bsz, s, B_GROUPS, B_INNER // B_GROUPS)).reshape(bsz, s, B_INNER)
    return y * w_norm.astype(F32)


def token_mixer(u, w_in, lb, gnorm, conv_w, conv_b, dt_bias, a_log, d_skip, ssm_norm,
                w_branch_a, w_branch_b, w_o):
    proj = u @ w_in
    offs = np.cumsum(IN_SPLITS)[:-1].tolist()
    q, f, i, g, z, xbc, dt, gate_a, gate_b = jnp.split(proj, offs, axis=-1)
    y_a = hgrn2_mixer(q, f, i, g, lb, gnorm).astype(u.dtype) @ w_branch_a
    y_b = mamba2_mixer(z, xbc, dt, conv_w, conv_b, dt_bias, a_log, d_skip,
                       ssm_norm).astype(u.dtype) @ w_branch_b
    merged = jax.nn.sigmoid(gate_a) * y_a + jax.nn.sigmoid(gate_b) * y_b
    return merged @ w_o


def swiglu(u, w_gate, w_up, w_down):
    return (jax.nn.silu(u @ w_gate) * (u @ w_up)) @ w_down


def setup_inputs(seed: int = 0) -> dict:
    key = jax.random.key(seed)
    ks = jax.random.split(key, 24)
    n = lambda k, shape, s: jax.random.normal(k, shape, F32) * s
    L, D = DEPTH, D_MODEL
    dt0 = jnp.exp(jax.random.uniform(ks[8], (L, B_HEADS), F32, np.log(1e-3), np.log(1e-1)))
    return {
        "x": jax.random.normal(ks[0], (BATCH, SEQ, D), F32),
        "c": jax.random.normal(ks[1], (BATCH, D), F32),
        "w_ada": n(ks[2], (L, D, 6 * D), D ** -0.5),
        "b_ada": n(ks[3], (L, 6 * D), 0.02),
        "w_in": n(ks[4], (L, D, IN_DIM), D ** -0.5),
        "hgrn_lb": n(ks[5], (DEPTH + 1, A_FDIM), 0.1),
        "hgrn_gnorm": 1.0 + n(ks[6], (L, A_HEAD_I), 0.02),
        "ssm_conv_w": n(ks[7], (L, B_CONV, B_CONV_DIM), B_CONV ** -0.5),
        "ssm_conv_b": n(ks[9], (L, B_CONV_DIM), 0.02),
        "ssm_dt_bias": dt0 + jnp.log(-jnp.expm1(-dt0)),
        "ssm_a_log": jnp.log(jax.random.uniform(ks[10], (L, B_HEADS), F32, 1.0, 16.0)),
        "ssm_d": 1.0 + n(ks[11], (L, B_HEADS), 0.02),
        "ssm_norm": 1.0 + n(ks[12], (L, B_INNER), 0.02),
        "w_branch_a": n(ks[13], (L, A_IDIM, D), A_IDIM ** -0.5),
        "w_branch_b": n(ks[14], (L, B_INNER, D), B_INNER ** -0.5),
        "w_o": n(ks[15], (L, D, D), BETA * D ** -0.5),
        "ln1_g": 1.0 + n(ks[16], (L, D), 0.02),
        "ln1_b": n(ks[17], (L, D), 0.02),
        "w_ffn_gate": n(ks[18], (L, D, D_FF), D ** -0.5),
        "w_ffn_up": n(ks[19], (L, D, D_FF), D ** -0.5),
        "w_ffn_down": n(ks[20], (L, D_FF, D), BETA * D_FF ** -0.5),
        "ln2_g": 1.0 + n(ks[21], (L, D), 0.02),
        "ln2_b": n(ks[22], (L, D), 0.02),
    }


def reference(x, c, w_ada, b_ada, w_in, hgrn_lb, hgrn_gnorm, ssm_conv_w, ssm_conv_b,
              ssm_dt_bias, ssm_a_log, ssm_d, ssm_norm, w_branch_a, w_branch_b, w_o,
              ln1_g, ln1_b, w_ffn_gate, w_ffn_up, w_ffn_down, ln2_g, ln2_b):
    cond = jax.nn.silu(c)
    lb_table = jnp.cumsum(jax.nn.softmax(hgrn_lb.astype(F32), axis=0), axis=0)
    for l in range(DEPTH):
        mod = (cond @ w_ada[l] + b_ada[l])[:, None, :]
        sh1, sc1, g1, sh2, sc2, g2 = jnp.split(mod, 6, axis=-1)
        u = layer_norm(x) * (1.0 + sc1) + sh1
        h = token_mixer(u, w_in[l], lb_table[l], hgrn_gnorm[l], ssm_conv_w[l], ssm_conv_b[l],
                        ssm_dt_bias[l], ssm_a_log[l], ssm_d[l], ssm_norm[l],
                        w_branch_a[l], w_branch_b[l], w_o[l])
        x = layer_norm(ALPHA * x + g1 * h, ln1_g[l], ln1_b[l])
        u = layer_norm(x) * (1.0 + sc2) + sh2
        h = swiglu(u, w_ffn_gate[l], w_ffn_up[l], w_ffn_down[l])
        x = layer_norm(ALPHA * x + g2 * h, ln2_g[l], ln2_b[l])
    return x
```

```python
import functools

import jax
import jax.numpy as jnp
from jax import lax
from jax.experimental import pallas as pl
from jax.experimental.pallas import tpu as pltpu

F32 = jnp.float32
BF16 = jnp.bfloat16

D_MODEL = 1024
DEPTH = 1
A_HEADS = 8
A_EXPAND = 128
A_HEAD_I = 128
B_INNER = 2048
B_HEADDIM = 64
B_HEADS = 32
B_GROUPS = 4
B_HG = 8
B_STATE = 128
B_CONV = 4
B_GROUP_W = B_INNER // B_GROUPS
D_FF = 2816
ALPHA = (2.0 * DEPTH) ** 0.25
LN_EPS = 1e-5
RMS_EPS = 1e-6

OFF_Q, OFF_F, OFF_I, OFF_G = 0, 1024, 2048, 3072
OFF_Z = 4096
OFF_X = 6144
OFF_B = OFF_X + B_INNER
OFF_C = OFF_B + B_GROUPS * B_STATE
OFF_GA = 9216
OFF_GB = 10240
MAIN_DIM = 11264
DT_PAD = 128

CHUNK = 64
SUB = 16
EXP_CLAMP = 80.0

VMEM_LIMIT = 56 * 1024 * 1024


def _cparams(sem):
    return pltpu.CompilerParams(dimension_semantics=sem, vmem_limit_bytes=VMEM_LIMIT)


def _layer_norm(x):
    mu = jnp.mean(x, axis=-1, keepdims=True)
    xc = x - mu
    return xc * lax.rsqrt(jnp.mean(xc * xc, axis=-1, keepdims=True) + LN_EPS)


def _silu(x):
    return x * jax.nn.sigmoid(x)


def _split_bf16(x):
    hi = x.astype(BF16)
    lo = (x - hi.astype(F32)).astype(BF16)
    return jnp.concatenate([hi, lo], axis=-1)


def _dot(a, b):
    return jnp.dot(a, b, preferred_element_type=F32)


def _dot_nt(a, b):
    return lax.dot_general(a, b, (((1,), (1,)), ((), ())), preferred_element_type=F32)


def _dot_tn(a, b):
    return lax.dot_general(a, b, (((0,), (0,)), ((), ())), preferred_element_type=F32)


def _ada_kernel(c_ref, w_ref, b_ref, o_ref):
    cond = _silu(c_ref[...])
    o_ref[...] = jnp.dot(cond, w_ref[...], preferred_element_type=F32,
                         precision=lax.Precision.HIGHEST) + b_ref[...]


def _ada(c, w_ada, b_ada):
    bsz, d = c.shape
    n = w_ada.shape[1]
    tn = 1536
    return pl.pallas_call(
        _ada_kernel,
        grid=(n // tn,),
        in_specs=[pl.BlockSpec((bsz, d), lambda j: (0, 0)),
                  pl.BlockSpec((d, tn), lambda j: (0, j)),
                  pl.BlockSpec((1, tn), lambda j: (0, j))],
        out_specs=pl.BlockSpec((bsz, tn), lambda j: (0, j)),
        out_shape=jax.ShapeDtypeStruct((bsz, n), F32),
        compiler_params=_cparams(("arbitrary",)),
        name="ada",
    )(c, w_ada, b_ada.reshape(1, n))


def _inproj_kernel(x_ref, mod_ref, w_ref, wdt_ref, o_ref, dt_ref, u_ref):
    @pl.when(pl.program_id(1) == 0)
    def _():
        m = mod_ref[0]
        u = _layer_norm(x_ref[...]) * (1.0 + m[1:2, :]) + m[0:1, :]
        ub = u.astype(BF16)
        u_ref[...] = ub
        dt_ref[...] = _dot(ub, wdt_ref[...])

    o_ref[...] = _dot(u_ref[...], w_ref[...])


def _in_proj(x2, mod3, w_main, w_dt, seq):
    n_tok, d = x2.shape
    tm, tn = 1024, 1024
    tiles_per_batch = seq // tm
    return pl.pallas_call(
        _inproj_kernel,
        grid=(n_tok // tm, MAIN_DIM // tn),
        in_specs=[pl.BlockSpec((tm, d), lambda i, j: (i, 0)),
                  pl.BlockSpec((1, 6, d), lambda i, j: (i // tiles_per_batch, 0, 0)),
                  pl.BlockSpec((d, tn), lambda i, j: (0, j)),
                  pl.BlockSpec((d, DT_PAD), lambda i, j: (0, 0))],
        out_specs=[pl.BlockSpec((tm, tn), lambda i, j: (i, j)),
                   pl.BlockSpec((tm, DT_PAD), lambda i, j: (i, 0))],
        out_shape=[jax.ShapeDtypeStruct((n_tok, MAIN_DIM), F32),
                   jax.ShapeDtypeStruct((n_tok, DT_PAD), F32)],
        scratch_shapes=[pltpu.VMEM((tm, d), BF16)],
        compiler_params=_cparams(("parallel", "arbitrary")),
        name="in_proj",
    )(x2, mod3, w_main, w_dt)


def _hgrn_head_chunk(q, fl, v, g, lb, gn, tril, st):
    f = lb + (1.0 - lb) * jax.nn.sigmoid(fl)
    lg = jnp.log(f)
    kk = 1.0 - f
    qf = _silu(q) * (A_EXPAND ** -0.5)
    bc2 = _dot(tril, _split_bf16(lg))
    bc = bc2[:, :A_EXPAND] + bc2[:, A_EXPAND:]
    vb = v.astype(BF16)

    o_rows = []
    qh_rows = []
    for blk in range(CHUNK // SUB):
        lo, hi = blk * SUB, (blk + 1) * SUB
        r = bc[lo - 1:lo, :] if blk > 0 else jnp.zeros((1, A_EXPAND), F32)
        qt = qf[lo:hi] * jnp.exp(bc[lo:hi] - r)
        kt = kk[:hi] * jnp.exp(jnp.minimum(r - bc[:hi], EXP_CLAMP))
        s = _dot_nt(qt.astype(BF16), kt.astype(BF16))
        row = lax.broadcasted_iota(jnp.int32, (SUB, hi), 0) + lo
        col = lax.broadcasted_iota(jnp.int32, (SUB, hi), 1)
        s = jnp.where(col <= row, s, 0.0)
        o_rows.append(_dot(s.astype(BF16), vb[:hi]))
        qh_rows.append(qt * jnp.exp(r))
    o = jnp.concatenate(o_rows, axis=0)
    qh = jnp.concatenate(qh_rows, axis=0)
    o = o + _dot_nt(qh.astype(BF16), st.astype(BF16))

    b_last = bc[CHUNK - 1:CHUNK, :]
    kh = kk * jnp.exp(b_last - bc)
    st_new = st * jnp.exp(b_last) + _dot_tn(vb, kh.astype(BF16))

    y = o * lax.rsqrt(jnp.mean(o * o, axis=-1, keepdims=True) + RMS_EPS) * gn
    return y * _silu(g), st_new


def _hgrn_kernel(q_ref, f_ref, i_ref, g_ref, lb_ref, gn_ref, o_ref, st_ref, *, n_chunks):
    @pl.when(pl.program_id(1) == 0)
    def _():
        st_ref[...] = jnp.zeros_like(st_ref)

    lbp = lb_ref[...]
    lbe = jnp.exp(lbp - jnp.max(lbp, axis=0, keepdims=True))
    lb_all = lbe[0:1, :] / jnp.sum(lbe, axis=0, keepdims=True)
    gn = gn_ref[...]
    tril = (lax.broadcasted_iota(jnp.int32, (CHUNK, CHUNK), 0)
            >= lax.broadcasted_iota(jnp.int32, (CHUNK, CHUNK), 1)).astype(BF16)

    def chunk_body(c, carry):
        rows = pl.ds(pl.multiple_of(c * CHUNK, CHUNK), CHUNK)
        for h in range(A_HEADS):
            cols = slice(h * A_EXPAND, (h + 1) * A_EXPAND)
            y, st_new = _hgrn_head_chunk(
                q_ref[0, rows, cols], f_ref[0, rows, cols], i_ref[0, rows, cols],
                g_ref[0, rows, cols], lb_all[:, cols], gn, tril, st_ref[h])
            st_ref[h] = st_new
            o_ref[0, rows, cols] = y.astype(o_ref.dtype)
        return carry

    lax.fori_loop(0, n_chunks, chunk_body, 0)


def _hgrn(proj3, hgrn_lb, gnorm):
    bsz, seq, _ = proj3.shape
    t = 256
    blk = lambda off: pl.BlockSpec((1, t, D_MODEL), lambda b, s, o=off // D_MODEL: (b, s, o))
    return pl.pallas_call(
        functools.partial(_hgrn_kernel, n_chunks=t // CHUNK),
        grid=(bsz, seq // t),
        in_specs=[blk(OFF_Q), blk(OFF_F), blk(OFF_I), blk(OFF_G),
                  pl.BlockSpec(hgrn_lb.shape, lambda b, s: (0, 0)),
                  pl.BlockSpec((1, A_HEAD_I), lambda b, s: (0, 0))],
        out_specs=pl.BlockSpec((1, t, D_MODEL), lambda b, s: (b, s, 0)),
        out_shape=jax.ShapeDtypeStruct((bsz, seq, D_MODEL), BF16),
        scratch_shapes=[pltpu.VMEM((A_HEADS, A_HEAD_I, A_EXPAND), F32)],
        compiler_params=_cparams(("parallel", "arbitrary")),
        name="hgrn",
    )(proj3, proj3, proj3, proj3, hgrn_lb, gnorm.reshape(1, A_HEAD_I))


def _conv_silu(pad_ref, x_ref, w_ref, b_ref, first):
    t = x_ref.shape[1]

    @pl.when(first)
    def _():
        pad_ref[0:8, :] = jnp.zeros((8, pad_ref.shape[1]), F32)

    @pl.when(jnp.logical_not(first))
    def _():
        pad_ref[0:8, :] = pad_ref[t:t + 8, :]

    pad_ref[8:t + 8, :] = x_ref[0]
    acc = b_ref[...] + w_ref[B_CONV - 1:B_CONV, :] * pad_ref[8:t + 8, :]
    for j in range(B_CONV - 1):
        off = 8 - (B_CONV - 1) + j
        acc = acc + w_ref[j:j + 1, :] * pad_ref[off:off + t, :]
    return _silu(acc)


def _ssd_kernel(x_ref, bm_ref, cm_ref, z_ref, dt_ref, wx_ref, wb_ref, wc_ref,
                bx_ref, bb_ref, bc_ref, dtb_ref, alog_ref, dsk_ref, nw_ref,
                o_ref,
                xpad, bpad, cpad, xc_ref, bcv_ref, ccv_ref, dl_ref, st_ref, at_ref, *, n_chunks):
    grp = pl.program_id(1)
    first = pl.program_id(2) == 0

    @pl.when(first)
    def _():
        st_ref[...] = jnp.zeros_like(st_ref)

    xc_ref[...] = _conv_silu(xpad, x_ref, wx_ref, bx_ref, first)
    bcv_ref[...] = _conv_silu(bpad, bm_ref, wb_ref, bb_ref, first)
    ccv_ref[...] = _conv_silu(cpad, cm_ref, wc_ref, bc_ref, first)
    dtv = dt_ref[0] + dtb_ref[...]
    dl_ref[...] = jnp.maximum(dtv, 0.0) + jnp.log1p(jnp.exp(-jnp.abs(dtv)))

    neg_a = -jnp.exp(alog_ref[...])
    hrow = lax.broadcasted_iota(jnp.int32, (DT_PAD, B_GROUP_W), 0)
    hcol = lax.broadcasted_iota(jnp.int32, (DT_PAD, B_GROUP_W), 1) // B_HEADDIM + grp * B_HG
    expand = (hrow == hcol).astype(BF16)
    tril = (lax.broadcasted_iota(jnp.int32, (CHUNK, CHUNK), 0)
            >= lax.broadcasted_iota(jnp.int32, (CHUNK, CHUNK), 1)).astype(BF16)
    lane = lax.broadcasted_iota(jnp.int32, (CHUNK, 2 * B_HEADDIM), 1)
    trow = lax.broadcasted_iota(jnp.int32, (CHUNK, 2 * B_HEADDIM), 0)
    left = lane < B_HEADDIM
    spos = jnp.where(left, lane, lane - B_HEADDIM)
    causal2 = trow >= spos
    triu2 = (trow <= spos).astype(BF16)
    exp2 = jnp.concatenate([expand, expand], axis=0)
    dsk = _dot(_split_bf16(jnp.broadcast_to(dsk_ref[...], (8, DT_PAD))), exp2)[0:1, :]

    def chunk_body(c, carry):
        rows = pl.ds(pl.multiple_of(c * CHUNK, CHUNK), CHUNK)
        xs = xc_ref[rows, :]
        bm = bcv_ref[rows, :].astype(BF16)
        cm = ccv_ref[rows, :].astype(BF16)
        delta = dl_ref[rows, :]
        a = delta * neg_a
        acum = _dot(tril, _split_bf16(a))
        acum = acum[:, :DT_PAD] + acum[:, DT_PAD:]
        acum_x = _dot(_split_bf16(acum), exp2)
        delta_x = _dot(_split_bf16(delta), exp2)
        a_last = acum_x[CHUNK - 1:CHUNK, :]
        xdt = xs * delta_x

        cb2 = _dot_nt(cm, jnp.concatenate([bm, bm], axis=0))
        a2 = _split_bf16(a)
        at_ref[...] = _dot_tn(a2[:, :DT_PAD], triu2) + _dot_tn(a2[:, DT_PAD:], triu2)
        at_g = at_ref[pl.ds(pl.multiple_of(grp * B_HG, B_HG), B_HG), :]
        y_tiles = []
        for p in range(B_GROUP_W // (2 * B_HEADDIM)):
            cs = slice(p * 2 * B_HEADDIM, (p + 1) * 2 * B_HEADDIM)
            rowpart = jnp.where(left, at_g[2 * p:2 * p + 1, :], at_g[2 * p + 1:2 * p + 2, :])
            seg = acum_x[:, cs] - rowpart
            decay = jnp.exp(jnp.where(causal2, seg, -jnp.inf))
            m = (cb2 * decay).astype(BF16)
            xp = xdt[:, cs]
            bd = jnp.concatenate([jnp.where(left, xp, 0.0), jnp.where(left, 0.0, xp)], axis=0)
            y_tiles.append(_dot(m, bd.astype(BF16)))
        y = jnp.concatenate(y_tiles, axis=1)

        st = st_ref[...]
        y = y + _dot(cm, st.astype(BF16)) * jnp.exp(acum_x)
        w = jnp.exp(a_last - acum_x)
        st_ref[...] = st * jnp.exp(a_last) + _dot_tn(bm, (xdt * w).astype(BF16))

        y = (y + xs * dsk) * _silu(z_ref[0, rows, :])
        y = y * lax.rsqrt(jnp.mean(y * y, axis=-1, keepdims=True) + RMS_EPS) * nw_ref[...]
        o_ref[0, rows, :] = y.astype(o_ref.dtype)
        return carry

    lax.fori_loop(0, n_chunks, chunk_body, 0)


def _ssd(proj3, dt3, conv_w, conv_b, dt_bias, a_log, d_skip, w_norm):
    bsz, seq, _ = proj3.shape
    t = 512
    gw, ns = B_GROUP_W, B_STATE
    pad32 = lambda v: jnp.pad(v.reshape(1, B_HEADS), ((0, 0), (0, DT_PAD - B_HEADS)))
    act = lambda width, off: pl.BlockSpec(
        (1, t, width), lambda b, g, s, o=off // width: (b, s, o + g))
    par = lambda rows, width, off: pl.BlockSpec(
        (rows, width), lambda b, g, s, o=off // width: (0, o + g))
    full = lambda shape: pl.BlockSpec(shape, lambda b, g, s: (0, 0))
    cb2 = conv_b.reshape(1, -1)
    coff_x, coff_b, coff_c = 0, B_INNER, B_INNER + B_GROUPS * B_STATE
    return pl.pallas_call(
        functools.partial(_ssd_kernel, n_chunks=t // CHUNK),
        grid=(bsz, B_GROUPS, seq // t),
        in_specs=[act(gw, OFF_X), act(ns, OFF_B), act(ns, OFF_C), act(gw, OFF_Z),
                  pl.BlockSpec((1, t, DT_PAD), lambda b, g, s: (b, s, 0)),
                  par(B_CONV, gw, coff_x), par(B_CONV, ns, coff_b), par(B_CONV, ns, coff_c),
                  par(1, gw, coff_x), par(1, ns, coff_b), par(1, ns, coff_c),
                  full((1, DT_PAD)), full((1, DT_PAD)), full((1, DT_PAD)),
                  par(1, gw, 0)],
        out_specs=pl.BlockSpec((1, t, gw), lambda b, g, s: (b, s, g)),
        out_shape=jax.ShapeDtypeStruct((bsz, seq, B_INNER), BF16),
        scratch_shapes=[pltpu.VMEM((t + 8, gw), F32), pltpu.VMEM((t + 8, ns), F32),
                        pltpu.VMEM((t + 8, ns), F32),
                        pltpu.VMEM((t, gw), F32), pltpu.VMEM((t, ns), F32),
                        pltpu.VMEM((t, ns), F32), pltpu.VMEM((t, DT_PAD), F32),
                        pltpu.VMEM((ns, gw), F32), pltpu.VMEM((DT_PAD, 2 * B_HEADDIM), F32)],
        compiler_params=_cparams(("parallel", "parallel", "arbitrary")),
        name="ssd",
    )(proj3, proj3, proj3, proj3, dt3, conv_w, conv_w, conv_w, cb2, cb2, cb2,
      pad32(dt_bias), pad32(a_log), pad32(d_skip), w_norm.reshape(1, B_INNER))


def _merge_kernel(ya_ref, yb_ref, ga_ref, gb_ref, x_ref, mod_ref, wa_ref, wb_ref, wo_ref,
                  lg_ref, lbias_ref, o_ref):
    m = mod_ref[0]
    pa = _dot(ya_ref[...], wa_ref[...])
    pb = _dot(yb_ref[...], wb_ref[...])
    merged = jax.nn.sigmoid(ga_ref[...]) * pa + jax.nn.sigmoid(gb_ref[...]) * pb
    h = _dot(merged.astype(BF16), wo_ref[...])
    y = _layer_norm(ALPHA * x_ref[...] + m[2:3, :] * h)
    o_ref[...] = y * lg_ref[...] + lbias_ref[...]


def _merge(ya2, yb2, proj2, x2, mod3, w_a, w_b, w_o, ln_g, ln_b, seq):
    n_tok, d = x2.shape
    tm = 512
    tiles_per_batch = seq // tm
    row = lambda width, off=0: pl.BlockSpec((tm, width), lambda i, o=off // width: (i, o))
    full = lambda shape: pl.BlockSpec(shape, lambda i: (0, 0))
    return pl.pallas_call(
        _merge_kernel,
        grid=(n_tok // tm,),
        in_specs=[row(d), row(B_INNER), row(d, OFF_GA), row(d, OFF_GB), row(d),
                  pl.BlockSpec((1, 6, d), lambda i: (i // tiles_per_batch, 0, 0)),
                  full((d, d)), full((B_INNER, d)), full((d, d)),
                  full((1, d)), full((1, d))],
        out_specs=row(d),
        out_shape=jax.ShapeDtypeStruct((n_tok, d), F32),
        compiler_params=_cparams(("parallel",)),
        name="merge",
    )(ya2, yb2, proj2, proj2, x2, mod3, w_a, w_b, w_o, ln_g.reshape(1, d), ln_b.reshape(1, d))


def _ffn_kernel(x_ref, mod_ref, wg_ref, wu_ref, wd_ref, lg_ref, lbias_ref, o_ref,
                u_ref, acc_ref):
    j = pl.program_id(1)
    m = mod_ref[0]

    @pl.when(j == 0)
    def _():
        u = _layer_norm(x_ref[...]) * (1.0 + m[4:5, :]) + m[3:4, :]
        u_ref[...] = u.astype(BF16)
        acc_ref[...] = jnp.zeros_like(acc_ref)

    u = u_ref[...]
    hidden = _silu(_dot(u, wg_ref[...])) * _dot(u, wu_ref[...])
    acc_ref[...] += _dot(hidden.astype(BF16), wd_ref[...])

    @pl.when(j == pl.num_programs(1) - 1)
    def _():
        y = _layer_norm(ALPHA * x_ref[...] + m[5:6, :] * acc_ref[...])
        o_ref[...] = y * lg_ref[...] + lbias_ref[...]


def _ffn(x2, mod3, w_gate, w_up, w_down, ln_g, ln_b, seq):
    n_tok, d = x2.shape
    tm, tf = 1024, D_FF // 2
    tiles_per_batch = seq // tm
    full = lambda shape: pl.BlockSpec(shape, lambda i, j: (0, 0))
    return pl.pallas_call(
        _ffn_kernel,
        grid=(n_tok // tm, D_FF // tf),
        in_specs=[pl.BlockSpec((tm, d), lambda i, j: (i, 0)),
                  pl.BlockSpec((1, 6, d), lambda i, j: (i // tiles_per_batch, 0, 0)),
                  pl.BlockSpec((d, tf), lambda i, j: (0, j)),
                  pl.BlockSpec((d, tf), lambda i, j: (0, j)),
                  pl.BlockSpec((tf, d), lambda i, j: (j, 0)),
                  full((1, d)), full((1, d))],
        out_specs=pl.BlockSpec((tm, d), lambda i, j: (i, 0)),
        out_shape=jax.ShapeDtypeStruct((n_tok, d), F32),
        scratch_shapes=[pltpu.VMEM((tm, d), BF16), pltpu.VMEM((tm, d), F32)],
        compiler_params=_cparams(("parallel", "arbitrary")),
        name="ffn",
    )(x2, mod3, w_gate, w_up, w_down, ln_g.reshape(1, d), ln_b.reshape(1, d))


def kernel(x, c, w_ada, b_ada, w_in, hgrn_lb, hgrn_gnorm, ssm_conv_w, ssm_conv_b, ssm_dt_bias,
           ssm_a_log, ssm_d, ssm_norm, w_branch_a, w_branch_b, w_o, ln1_g, ln1_b,
           w_ffn_gate, w_ffn_up, w_ffn_down, ln2_g, ln2_b):
    bsz, seq, d = x.shape
    assert d == D_MODEL and w_ada.shape[0] == DEPTH == 1
    x2 = x.reshape(bsz * seq, d)
    for l in range(DEPTH):
        mod3 = _ada(c, w_ada[l], b_ada[l]).reshape(bsz, 6, d)
        w = w_in[l]
        dt_lo = OFF_X + B_INNER + 2 * B_GROUPS * B_STATE
        w_main = jnp.concatenate([w[:, :dt_lo], w[:, dt_lo + B_HEADS:]], axis=1).astype(BF16)
        w_dt = jnp.pad(w[:, dt_lo:dt_lo + B_HEADS], ((0, 0), (0, DT_PAD - B_HEADS))).astype(BF16)
        proj2, dt2 = _in_proj(x2, mod3, w_main, w_dt, seq)
        proj3 = proj2.reshape(bsz, seq, MAIN_DIM)
        ya = _hgrn(proj3, hgrn_lb, hgrn_gnorm[l])
        yb = _ssd(proj3, dt2.reshape(bsz, seq, DT_PAD), ssm_conv_w[l], ssm_conv_b[l],
                  ssm_dt_bias[l], ssm_a_log[l], ssm_d[l], ssm_norm[l])
        x2 = _merge(ya.reshape(bsz * seq, d), yb.reshape(bsz * seq, B_INNER), proj2, x2, mod3,
                    w_branch_a[l].astype(BF16), w_branch_b[l].astype(BF16), w_o[l].astype(BF16),
                    ln1_g[l], ln1_b[l], seq)
        x2 = _ffn(x2, mod3, w_ffn_gate[l].astype(BF16), w_ffn_up[l].astype(BF16),
                  w_ffn_down[l].astype(BF16), ln2_g[l], ln2_b[l], seq)
    return x2.reshape(bsz, seq, d)
```

```python
import functools

import jax
import jax.numpy as jnp
from jax import lax
from jax.experimental import pallas as pl
from jax.experimental.pallas import tpu as pltpu

F32 = jnp.float32
BF16 = jnp.bfloat16

D_MODEL = 1024
DEPTH = 1
A_HEADS = 8
A_EXPAND = 128
A_HEAD_I = 128
B_INNER = 2048
B_HEADDIM = 64
B_HEADS = 32
B_GROUPS = 4
B_HG = 8
B_STATE = 128
B_CONV = 4
B_GROUP_W = B_INNER // B_GROUPS
D_FF = 2816
ALPHA = (2.0 * DEPTH) ** 0.25
LN_EPS = 1e-5
RMS_EPS = 1e-6

OFF_Q, OFF_F, OFF_I, OFF_G = 0, 1024, 2048, 3072
OFF_Z = 4096
OFF_X = 6144
OFF_B = OFF_X + B_INNER
OFF_C = OFF_B + B_GROUPS * B_STATE
OFF_GA = 9216
OFF_GB = 10240
MAIN_DIM = 11264
DT_PAD = 128

CHUNK = 64
SUB = 16
N_SUB = CHUNK // SUB
SEC_OFF = tuple(SUB * i * (i + 1) // 2 for i in range(N_SUB))
STACK = SUB * N_SUB * (N_SUB + 1) // 2
EXP_CLAMP = 80.0

VMEM_LIMIT = 56 * 1024 * 1024


def _cparams(sem):
    return pltpu.CompilerParams(dimension_semantics=sem, vmem_limit_bytes=VMEM_LIMIT)


def _layer_norm(x):
    mu = jnp.mean(x, axis=-1, keepdims=True)
    xc = x - mu
    return xc * lax.rsqrt(jnp.mean(xc * xc, axis=-1, keepdims=True) + LN_EPS)


def _silu(x):
    return x * jax.nn.sigmoid(x)


def _split_bf16(x, axis=-1):
    hi = x.astype(BF16)
    lo = (x - hi.astype(F32)).astype(BF16)
    return jnp.concatenate([hi, lo], axis=axis)


def _dot(a, b):
    return jnp.dot(a, b, preferred_element_type=F32)


def _dot_nt(a, b):
    return lax.dot_general(a, b, (((1,), (1,)), ((), ())), preferred_element_type=F32)


def _dot_tn(a, b):
    return lax.dot_general(a, b, (((0,), (0,)), ((), ())), preferred_element_type=F32)


def _tril_bf16(n):
    return (lax.broadcasted_iota(jnp.int32, (n, n), 0)
            >= lax.broadcasted_iota(jnp.int32, (n, n), 1)).astype(BF16)


def _cumsum_rows(tril, x):
    w = x.shape[1]
    y = _dot(tril, _split_bf16(x))
    return y[:, :w] + y[:, w:]


def _ada_kernel(c_ref, w_ref, b_ref, o_ref):
    cond = _silu(c_ref[...])
    o_ref[...] = jnp.dot(cond, w_ref[...], preferred_element_type=F32,
                         precision=lax.Precision.HIGHEST) + b_ref[...]


def _ada(c, w_ada, b_ada):
    bsz, d = c.shape
    n = w_ada.shape[1]
    tn = 1536
    return pl.pallas_call(
        _ada_kernel,
        grid=(n // tn,),
        in_specs=[pl.BlockSpec((bsz, d), lambda j: (0, 0)),
                  pl.BlockSpec((d, tn), lambda j: (0, j)),
                  pl.BlockSpec((1, tn), lambda j: (0, j))],
        out_specs=pl.BlockSpec((bsz, tn), lambda j: (0, j)),
        out_shape=jax.ShapeDtypeStruct((bsz, n), F32),
        compiler_params=_cparams(("arbitrary",)),
        name="ada",
    )(c, w_ada, b_ada.reshape(1, n))


def _inproj_kernel(x_ref, mod_ref, w_ref, wdt_ref, o_ref, dt_ref, u_ref):
    @pl.when(pl.program_id(1) == 0)
    def _():
        m = mod_ref[0]
        u = _layer_norm(x_ref[...]) * (1.0 + m[1:2, :]) + m[0:1, :]
        ub = u.astype(BF16)
        u_ref[...] = ub
        dt_ref[...] = _dot(ub, wdt_ref[...])

    o_ref[...] = _dot(u_ref[...], w_ref[...])


def _in_proj(x2, mod3, w_main, w_dt, seq):
    n_tok, d = x2.shape
    tm, tn = 1024, 1024
    tiles_per_batch = seq // tm
    return pl.pallas_call(
        _inproj_kernel,
        grid=(n_tok // tm, MAIN_DIM // tn),
        in_specs=[pl.BlockSpec((tm, d), lambda i, j: (i, 0)),
                  pl.BlockSpec((1, 6, d), lambda i, j: (i // tiles_per_batch, 0, 0)),
                  pl.BlockSpec((d, tn), lambda i, j: (0, j)),
                  pl.BlockSpec((d, DT_PAD), lambda i, j: (0, 0))],
        out_specs=[pl.BlockSpec((tm, tn), lambda i, j: (i, j)),
                   pl.BlockSpec((tm, DT_PAD), lambda i, j: (i, 0))],
        out_shape=[jax.ShapeDtypeStruct((n_tok, MAIN_DIM), F32),
                   jax.ShapeDtypeStruct((n_tok, DT_PAD), F32)],
        scratch_shapes=[pltpu.VMEM((tm, d), BF16)],
        compiler_params=_cparams(("parallel", "arbitrary")),
        name="in_proj",
    )(x2, mod3, w_main, w_dt)


def _hgrn_kernel(q_ref, f_ref, i_ref, g_ref, lb_ref, gn_ref, o_ref,
                 st_ref, qt_ref, qh_ref, kh_ref, kall_ref, vall_ref, dec_ref, oacc_ref,
                 *, n_chunks):
    @pl.when(pl.program_id(1) == 0)
    def _():
        st_ref[...] = jnp.zeros_like(st_ref)

    lbp = lb_ref[...]
    lbe = jnp.exp(lbp - jnp.max(lbp, axis=0, keepdims=True))
    lb = lbe[0:1, :] / jnp.sum(lbe, axis=0, keepdims=True)
    gn = gn_ref[...]
    tril = _tril_bf16(CHUNK)

    def prep(c, carry):
        r0 = c * CHUNK
        s0 = c * STACK
        rows = pl.ds(pl.multiple_of(r0, CHUNK), CHUNK)
        f = lb + (1.0 - lb) * jax.nn.sigmoid(f_ref[0, rows, :])
        kk = 1.0 - f
        bc = _cumsum_rows(tril, jnp.log(f))
        qf = _silu(q_ref[0, rows, :]) * (A_EXPAND ** -0.5)
        vb = i_ref[0, rows, :].astype(BF16)
        for blk in range(N_SUB):
            lo, hi = blk * SUB, (blk + 1) * SUB
            r = bc[lo - 1:lo, :] if blk > 0 else jnp.zeros((1, bc.shape[1]), F32)
            qt = qf[lo:hi] * jnp.exp(bc[lo:hi] - r)
            sub_rows = pl.ds(pl.multiple_of(r0 + lo, SUB), SUB)
            qt_ref[sub_rows, :] = qt.astype(BF16)
            qh_ref[sub_rows, :] = (qt * jnp.exp(r)).astype(BF16)
            sec = pl.ds(pl.multiple_of(s0 + SEC_OFF[blk], SUB), hi)
            kall_ref[sec, :] = (kk[:hi] * jnp.exp(jnp.minimum(r - bc[:hi], EXP_CLAMP))).astype(BF16)
            vall_ref[sec, :] = vb[:hi]
        b_last = bc[CHUNK - 1:CHUNK, :]
        kh_ref[rows, :] = (kk * jnp.exp(b_last - bc)).astype(BF16)
        dec_ref[pl.ds(pl.multiple_of(c * 8, 8), 8), :] = jnp.broadcast_to(
            jnp.exp(b_last), (8, b_last.shape[1]))
        return carry

    lax.fori_loop(0, n_chunks, prep, 0)

    trow = lax.broadcasted_iota(jnp.int32, (CHUNK, STACK), 0)
    col = lax.broadcasted_iota(jnp.int32, (CHUNK, STACK), 1)
    valid = None
    for blk in range(N_SUB):
        in_sec = (col >= SEC_OFF[blk]) & (col < SEC_OFF[blk] + SUB * (blk + 1))
        ok = in_sec & (trow >= blk * SUB) & (trow < (blk + 1) * SUB) & (col - SEC_OFF[blk] <= trow)
        valid = ok if valid is None else (valid | ok)

    def scan(c, carry):
        rows = pl.ds(pl.multiple_of(c * CHUNK, CHUNK), CHUNK)
        srows = pl.ds(pl.multiple_of(c * STACK, SUB), STACK)
        vrows = pl.ds(pl.multiple_of(c * STACK + SEC_OFF[-1], SUB), CHUNK)
        drow = pl.ds(c * 8, 1)
        heads = [slice(h * A_EXPAND, (h + 1) * A_EXPAND) for h in range(A_HEADS)]
        sc = [_dot_nt(qt_ref[rows, cs], kall_ref[srows, cs]) for cs in heads]
        st = [st_ref[h] for h in range(A_HEADS)]
        inter = [_dot_nt(qh_ref[rows, cs], st[h].astype(BF16)) for h, cs in enumerate(heads)]
        upd = [_dot_tn(vall_ref[vrows, cs], kh_ref[rows, cs]) for cs in heads]
        out = [_dot(jnp.where(valid, sc[h], 0.0).astype(BF16), vall_ref[srows, cs]) + inter[h]
               for h, cs in enumerate(heads)]
        new_st = [st[h] * dec_ref[drow, cs] + upd[h] for h, cs in enumerate(heads)]
        for h, cs in enumerate(heads):
            oacc_ref[rows, cs] = out[h]
            st_ref[h] = new_st[h]
        return carry

    lax.fori_loop(0, n_chunks, scan, 0)

    def fin(c, carry):
        rows = pl.ds(pl.multiple_of(c * CHUNK, CHUNK), CHUNK)
        for h in range(A_HEADS):
            cols = slice(h * A_EXPAND, (h + 1) * A_EXPAND)
            o = oacc_ref[rows, cols]
            y = o * lax.rsqrt(jnp.mean(o * o, axis=-1, keepdims=True) + RMS_EPS) * gn
            o_ref[0, rows, cols] = (y * _silu(g_ref[0, rows, cols])).astype(o_ref.dtype)
        return carry

    lax.fori_loop(0, n_chunks, fin, 0)


def _hgrn(proj3, hgrn_lb, gnorm, t=256):
    bsz, seq, _ = proj3.shape
    nc = t // CHUNK
    blk = lambda off: pl.BlockSpec((1, t, D_MODEL), lambda b, s, o=off // D_MODEL: (b, s, o))
    return pl.pallas_call(
        functools.partial(_hgrn_kernel, n_chunks=nc),
        grid=(bsz, seq // t),
        in_specs=[blk(OFF_Q), blk(OFF_F), blk(OFF_I), blk(OFF_G),
                  pl.BlockSpec(hgrn_lb.shape, lambda b, s: (0, 0)),
                  pl.BlockSpec((1, A_HEAD_I), lambda b, s: (0, 0))],
        out_specs=pl.BlockSpec((1, t, D_MODEL), lambda b, s: (b, s, 0)),
        out_shape=jax.ShapeDtypeStruct((bsz, seq, D_MODEL), BF16),
        scratch_shapes=[pltpu.VMEM((A_HEADS, A_HEAD_I, A_EXPAND), F32),
                        pltpu.VMEM((t, D_MODEL), BF16),
                        pltpu.VMEM((t, D_MODEL), BF16),
                        pltpu.VMEM((t, D_MODEL), BF16),
                        pltpu.VMEM((nc * STACK, D_MODEL), BF16),
                        pltpu.VMEM((nc * STACK, D_MODEL), BF16),
                        pltpu.VMEM((nc * 8, D_MODEL), F32),
                        pltpu.VMEM((t, D_MODEL), F32)],
        compiler_params=_cparams(("parallel", "arbitrary")),
        name="hgrn",
    )(proj3, proj3, proj3, proj3, hgrn_lb, gnorm.reshape(1, A_HEAD_I))


def _conv_silu(pad_ref, x_ref, w_ref, b_ref, first):
    t = x_ref.shape[1]

    @pl.when(first)
    def _():
        pad_ref[0:8, :] = jnp.zeros((8, pad_ref.shape[1]), F32)

    @pl.when(jnp.logical_not(first))
    def _():
        pad_ref[0:8, :] = pad_ref[t:t + 8, :]

    pad_ref[8:t + 8, :] = x_ref[0]
    acc = b_ref[...] + w_ref[B_CONV - 1:B_CONV, :] * pad_ref[8:t + 8, :]
    for j in range(B_CONV - 1):
        off = 8 - (B_CONV - 1) + j
        acc = acc + w_ref[j:j + 1, :] * pad_ref[off:off + t, :]
    return _silu(acc)


def _ssd_kernel(x_ref, bm_ref, cm_ref, z_ref, dt_ref, wx_ref, wb_ref, wc_ref,
                bx_ref, bb_ref, bc_ref, dtb_ref, alog_ref, dsk_ref, nw_ref,
                o_ref,
                xpad, bpad, cpad, xc_ref, bcv_ref, ccv_ref, xdt_ref, ax_ref,
                y_ref, st_ref, *, n_chunks):
    grp = pl.program_id(1)
    first = pl.program_id(2) == 0

    @pl.when(first)
    def _():
        st_ref[...] = jnp.zeros_like(st_ref)

    hrow = lax.broadcasted_iota(jnp.int32, (DT_PAD, B_GROUP_W), 0)
    hcol = lax.broadcasted_iota(jnp.int32, (DT_PAD, B_GROUP_W), 1) // B_HEADDIM + grp * B_HG
    expand = (hrow == hcol).astype(BF16)
    exp2 = jnp.concatenate([expand, expand], axis=0)

    def expand_row(ref):
        return _dot(_split_bf16(jnp.broadcast_to(ref[...], (8, DT_PAD))), exp2)[0:1, :]

    xc = _conv_silu(xpad, x_ref, wx_ref, bx_ref, first)
    xc_ref[...] = xc
    bcv_ref[...] = _conv_silu(bpad, bm_ref, wb_ref, bb_ref, first).astype(BF16)
    ccv_ref[...] = _conv_silu(cpad, cm_ref, wc_ref, bc_ref, first).astype(BF16)
    dtv = dt_ref[0] + dtb_ref[...]
    delta = jnp.maximum(dtv, 0.0) + jnp.log1p(jnp.exp(-jnp.abs(dtv)))
    delta_x = _dot(_split_bf16(delta), exp2)
    ax_ref[...] = delta_x * (-jnp.exp(expand_row(alog_ref)))
    xdt_ref[...] = xc * delta_x

    tril = _tril_bf16(CHUNK)
    lane = lax.broadcasted_iota(jnp.int32, (CHUNK, 2 * B_HEADDIM), 1)
    trow = lax.broadcasted_iota(jnp.int32, (CHUNK, 2 * B_HEADDIM), 0)
    left = lane < B_HEADDIM
    spos = jnp.where(left, lane, lane - B_HEADDIM)
    causal2 = trow >= spos
    n_pairs = B_GROUP_W // (2 * B_HEADDIM)
    keep = jnp.tile((trow <= spos).astype(BF16), (1, 2 * n_pairs))
    ones8 = jnp.ones((8, CHUNK), BF16)

    def chunk_body(c, carry):
        rows = pl.ds(pl.multiple_of(c * CHUNK, CHUNK), CHUNK)
        bm = bcv_ref[rows, :]
        cm = ccv_ref[rows, :]
        xdt = xdt_ref[rows, :]
        axs = _split_bf16(ax_ref[rows, :])
        acx2 = _dot(tril, axs)
        acx = acx2[:, :B_GROUP_W] + acx2[:, B_GROUP_W:]
        a_last = acx[CHUNK - 1:CHUNK, :]
        kr2 = _dot(ones8, axs * keep)
        krow = kr2[0:1, :B_GROUP_W] + kr2[0:1, B_GROUP_W:]
        seg = acx - krow

        cb2 = _dot_nt(cm, jnp.concatenate([bm, bm], axis=0))
        st = st_ref[...]
        inter = _dot(cm, st.astype(BF16)) * jnp.exp(acx)
        xw = (xdt * jnp.exp(a_last - acx)).astype(BF16)
        new_st = st * jnp.exp(a_last) + _dot_tn(bm, xw)
        y_tiles = []
        for p in range(n_pairs):
            cs = slice(p * 2 * B_HEADDIM, (p + 1) * 2 * B_HEADDIM)
            decay = jnp.exp(jnp.where(causal2, seg[:, cs], -jnp.inf))
            m = (cb2 * decay).astype(BF16)
            xp = xdt[:, cs]
            bd = jnp.concatenate([jnp.where(left, xp, 0.0), jnp.where(left, 0.0, xp)], axis=0)
            y_tiles.append(_dot(m, bd.astype(BF16)) + inter[:, cs])
        y_ref[rows, :] = jnp.concatenate(y_tiles, axis=1)
        st_ref[...] = new_st
        return carry

    lax.fori_loop(0, n_chunks, chunk_body, 0)

    dsk = expand_row(dsk_ref)

    def fin(c, carry):
        rows = pl.ds(pl.multiple_of(c * CHUNK, CHUNK), CHUNK)
        y = (y_ref[rows, :] + xc_ref[rows, :] * dsk) * _silu(z_ref[0, rows, :])
        y = y * lax.rsqrt(jnp.mean(y * y, axis=-1, keepdims=True) + RMS_EPS) * nw_ref[...]
        o_ref[0, rows, :] = y.astype(o_ref.dtype)
        return carry

    lax.fori_loop(0, n_chunks, fin, 0)


def _ssd(proj3, dt3, conv_w, conv_b, dt_bias, a_log, d_skip, w_norm, t=512):
    bsz, seq, _ = proj3.shape
    nc = t // CHUNK
    gw, ns = B_GROUP_W, B_STATE
    pad32 = lambda v: jnp.pad(v.reshape(1, B_HEADS), ((0, 0), (0, DT_PAD - B_HEADS)))
    act = lambda width, off: pl.BlockSpec(
        (1, t, width), lambda b, g, s, o=off // width: (b, s, o + g))
    par = lambda rows, width, off: pl.BlockSpec(
        (rows, width), lambda b, g, s, o=off // width: (0, o + g))
    full = lambda shape: pl.BlockSpec(shape, lambda b, g, s: (0, 0))
    cb2 = conv_b.reshape(1, -1)
    coff_x, coff_b, coff_c = 0, B_INNER, B_INNER + B_GROUPS * B_STATE
    return pl.pallas_call(
        functools.partial(_ssd_kernel, n_chunks=nc),
        grid=(bsz, B_GROUPS, seq // t),
        in_specs=[act(gw, OFF_X), act(ns, OFF_B), act(ns, OFF_C), act(gw, OFF_Z),
                  pl.BlockSpec((1, t, DT_PAD), lambda b, g, s: (b, s, 0)),
                  par(B_CONV, gw, coff_x), par(B_CONV, ns, coff_b), par(B_CONV, ns, coff_c),
                  par(1, gw, coff_x), par(1, ns, coff_b), par(1, ns, coff_c),
                  full((1, DT_PAD)), full((1, DT_PAD)), full((1, DT_PAD)),
                  par(1, gw, 0)],
        out_specs=pl.BlockSpec((1, t, gw), lambda b, g, s: (b, s, g)),
        out_shape=jax.ShapeDtypeStruct((bsz, seq, B_INNER), BF16),
        scratch_shapes=[pltpu.VMEM((t + 8, gw), F32), pltpu.VMEM((t + 8, ns), F32),
                        pltpu.VMEM((t + 8, ns), F32),
                        pltpu.VMEM((t, gw), F32),
                        pltpu.VMEM((t, ns), BF16),
                        pltpu.VMEM((t, ns), BF16),
                        pltpu.VMEM((t, gw), F32),
                        pltpu.VMEM((t, gw), F32),
                        pltpu.VMEM((t, gw), F32),
                        pltpu.VMEM((ns, gw), F32)],
        compiler_params=_cparams(("parallel", "parallel", "arbitrary")),
        name="ssd",
    )(proj3, proj3, proj3, proj3, dt3, conv_w, conv_w, conv_w, cb2, cb2, cb2,
      pad32(dt_bias), pad32(a_log), pad32(d_skip), w_norm.reshape(1, B_INNER))


def _merge_kernel(ya_ref, yb_ref, ga_ref, gb_ref, x_ref, mod_ref, wa_ref, wb_ref, wo_ref,
                  lg_ref, lbias_ref, o_ref):
    m = mod_ref[0]
    pa = _dot(ya_ref[...], wa_ref[...])
    pb = _dot(yb_ref[...], wb_ref[...])
    merged = jax.nn.sigmoid(ga_ref[...]) * pa + jax.nn.sigmoid(gb_ref[...]) * pb
    h = _dot(merged.astype(BF16), wo_ref[...])
    y = _layer_norm(ALPHA * x_ref[...] + m[2:3, :] * h)
    o_ref[...] = y * lg_ref[...] + lbias_ref[...]


def _merge(ya2, yb2, proj2, x2, mod3, w_a, w_b, w_o, ln_g, ln_b, seq):
    n_tok, d = x2.shape
    tm = 512
    tiles_per_batch = seq // tm
    row = lambda width, off=0: pl.BlockSpec((tm, width), lambda i, o=off // width: (i, o))
    full = lambda shape: pl.BlockSpec(shape, lambda i: (0, 0))
    return pl.pallas_call(
        _merge_kernel,
        grid=(n_tok // tm,),
        in_specs=[row(d), row(B_INNER), row(d, OFF_GA), row(d, OFF_GB), row(d),
                  pl.BlockSpec((1, 6, d), lambda i: (i // tiles_per_batch, 0, 0)),
                  full((d, d)), full((B_INNER, d)), full((d, d)),
                  full((1, d)), full((1, d))],
        out_specs=row(d),
        out_shape=jax.ShapeDtypeStruct((n_tok, d), F32),
        compiler_params=_cparams(("parallel",)),
        name="merge",
    )(ya2, yb2, proj2, proj2, x2, mod3, w_a, w_b, w_o, ln_g.reshape(1, d), ln_b.reshape(1, d))


def _ffn_kernel(x_ref, mod_ref, wg_ref, wu_ref, wd_ref, lg_ref, lbias_ref, o_ref,
                u_ref, acc_ref):
    j = pl.program_id(1)
    m = mod_ref[0]

    @pl.when(j == 0)
    def _():
        u = _layer_norm(x_ref[...]) * (1.0 + m[4:5, :]) + m[3:4, :]
        u_ref[...] = u.astype(BF16)
        acc_ref[...] = jnp.zeros_like(acc_ref)

    u = u_ref[...]
    hidden = _silu(_dot(u, wg_ref[...])) * _dot(u, wu_ref[...])
    acc_ref[...] += _dot(hidden.astype(BF16), wd_ref[...])

    @pl.when(j == pl.num_programs(1) - 1)
    def _():
        y = _layer_norm(ALPHA * x_ref[...] + m[5:6, :] * acc_ref[...])
        o_ref[...] = y * lg_ref[...] + lbias_ref[...]


def _ffn(x2, mod3, w_gate, w_up, w_down, ln_g, ln_b, seq):
    n_tok, d = x2.shape
    tm, tf = 1024, D_FF // 2
    tiles_per_batch = seq // tm
    full = lambda shape: pl.BlockSpec(shape, lambda i, j: (0, 0))
    return pl.pallas_call(
        _ffn_kernel,
        grid=(n_tok // tm, D_FF // tf),
        in_specs=[pl.BlockSpec((tm, d), lambda i, j: (i, 0)),
                  pl.BlockSpec((1, 6, d), lambda i, j: (i // tiles_per_batch, 0, 0)),
                  pl.BlockSpec((d, tf), lambda i, j: (0, j)),
                  pl.BlockSpec((d, tf), lambda i, j: (0, j)),
                  pl.BlockSpec((tf, d), lambda i, j: (j, 0)),
                  full((1, d)), full((1, d))],
        out_specs=pl.BlockSpec((tm, d), lambda i, j: (i, 0)),
        out_shape=jax.ShapeDtypeStruct((n_tok, d), F32),
        scratch_shapes=[pltpu.VMEM((tm, d), BF16), pltpu.VMEM((tm, d), F32)],
        compiler_params=_cparams(("parallel", "arbitrary")),
        name="ffn",
    )(x2, mod3, w_gate, w_up, w_down, ln_g.reshape(1, d), ln_b.reshape(1, d))


def kernel(x, c, w_ada, b_ada, w_in, hgrn_lb, hgrn_gnorm, ssm_conv_w, ssm_conv_b, ssm_dt_bias,
           ssm_a_log, ssm_d, ssm_norm, w_branch_a, w_branch_b, w_o, ln1_g, ln1_b,
           w_ffn_gate, w_ffn_up, w_ffn_down, ln2_g, ln2_b):
    bsz, seq, d = x.shape
    assert d == D_MODEL and w_ada.shape[0] == DEPTH == 1
    x2 = x.reshape(bsz * seq, d)
    for l in range(DEPTH):
        mod3 = _ada(c, w_ada[l], b_ada[l]).reshape(bsz, 6, d)
        w = w_in[l]
        dt_lo = OFF_X + B_INNER + 2 * B_GROUPS * B_STATE
        w_main = jnp.concatenate([w[:, :dt_lo], w[:, dt_lo + B_HEADS:]], axis=1).astype(BF16)
        w_dt = jnp.pad(w[:, dt_lo:dt_lo + B_HEADS], ((0, 0), (0, DT_PAD - B_HEADS))).astype(BF16)
        proj2, dt2 = _in_proj(x2, mod3, w_main, w_dt, seq)
        proj3 = proj2.reshape(bsz, seq, MAIN_DIM)
        ya = _hgrn(proj3, hgrn_lb, hgrn_gnorm[l])
        yb = _ssd(proj3, dt2.reshape(bsz, seq, DT_PAD), ssm_conv_w[l], ssm_conv_b[l],
                  ssm_dt_bias[l], ssm_a_log[l], ssm_d[l], ssm_norm[l])
        x2 = _merge(ya.reshape(bsz * seq, d), yb.reshape(bsz * seq, B_INNER), proj2, x2, mod3,
                    w_branch_a[l].astype(BF16), w_branch_b[l].astype(BF16), w_o[l].astype(BF16),
                    ln1_g[l], ln1_b[l], seq)
        x2 = _ffn(x2, mod3, w_ffn_gate[l].astype(BF16), w_ffn_up[l].astype(BF16),
                  w_ffn_down[l].astype(BF16), ln2_g[l], ln2_b[l], seq)
    return x2.reshape(bsz, seq, d)
```

```python
import functools

import jax
import jax.numpy as jnp
from jax import lax
from jax.experimental import pallas as pl
from jax.experimental.pallas import tpu as pltpu

F32 = jnp.float32
BF16 = jnp.bfloat16

D_MODEL = 1024
DEPTH = 1
A_HEADS = 8
A_EXPAND = 128
A_HEAD_I = 128
B_INNER = 2048
B_HEADDIM = 64
B_HEADS = 32
B_GROUPS = 4
B_HG = 8
B_STATE = 128
B_CONV = 4
B_GROUP_W = B_INNER // B_GROUPS
D_FF = 2816
ALPHA = (2.0 * DEPTH) ** 0.25
LN_EPS = 1e-5
RMS_EPS = 1e-6

OFF_Q, OFF_F, OFF_I, OFF_G = 0, 1024, 2048, 3072
OFF_Z = 4096
OFF_X = 6144
OFF_B = OFF_X + B_INNER
OFF_C = OFF_B + B_GROUPS * B_STATE
OFF_GA = 9216
OFF_GB = 10240
MAIN_DIM = 11264
DT_PAD = 128

CHUNK = 64
SUB = 16
N_SUB = CHUNK // SUB
SEC_OFF = tuple(SUB * i * (i + 1) // 2 for i in range(N_SUB))
STACK = SUB * N_SUB * (N_SUB + 1) // 2
EXP_CLAMP = 80.0
SSD_CHUNKS_PER_STEP = 8

VMEM_LIMIT = 56 * 1024 * 1024


def _cparams(sem):
    return pltpu.CompilerParams(dimension_semantics=sem, vmem_limit_bytes=VMEM_LIMIT)


def _layer_norm(x):
    mu = jnp.mean(x, axis=-1, keepdims=True)
    xc = x - mu
    return xc * lax.rsqrt(jnp.mean(xc * xc, axis=-1, keepdims=True) + LN_EPS)


def _silu(x):
    return x * jax.nn.sigmoid(x)


def _split_bf16(x, axis=-1):
    hi = x.astype(BF16)
    lo = (x - hi.astype(F32)).astype(BF16)
    return jnp.concatenate([hi, lo], axis=axis)


def _dot(a, b):
    return jnp.dot(a, b, preferred_element_type=F32)


def _dot_nt(a, b):
    return lax.dot_general(a, b, (((1,), (1,)), ((), ())), preferred_element_type=F32)


def _dot_tn(a, b):
    return lax.dot_general(a, b, (((0,), (0,)), ((), ())), preferred_element_type=F32)


def _ds(start, size, align):
    if isinstance(start, int):
        return pl.ds(start, size)
    return pl.ds(pl.multiple_of(start, align), size)


def _tril_bf16(n):
    return (lax.broadcasted_iota(jnp.int32, (n, n), 0)
            >= lax.broadcasted_iota(jnp.int32, (n, n), 1)).astype(BF16)


def _cumsum_rows(tril, x):
    w = x.shape[1]
    y = _dot(tril, _split_bf16(x))
    return y[:, :w] + y[:, w:]


def _ada_kernel(c_ref, w_ref, b_ref, o_ref):
    cond = _silu(c_ref[...])
    o_ref[...] = jnp.dot(cond, w_ref[...], preferred_element_type=F32,
                         precision=lax.Precision.HIGHEST) + b_ref[...]


def _ada(c, w_ada, b_ada):
    bsz, d = c.shape
    n = w_ada.shape[1]
    tn = 1536
    return pl.pallas_call(
        _ada_kernel,
        grid=(n // tn,),
        in_specs=[pl.BlockSpec((bsz, d), lambda j: (0, 0)),
                  pl.BlockSpec((d, tn), lambda j: (0, j)),
                  pl.BlockSpec((1, tn), lambda j: (0, j))],
        out_specs=pl.BlockSpec((bsz, tn), lambda j: (0, j)),
        out_shape=jax.ShapeDtypeStruct((bsz, n), F32),
        compiler_params=_cparams(("arbitrary",)),
        name="ada",
    )(c, w_ada, b_ada.reshape(1, n))


def _inproj_kernel(x_ref, mod_ref, w_ref, wdt_ref, o_ref, dt_ref, u_ref):
    @pl.when(pl.program_id(1) == 0)
    def _():
        m = mod_ref[0]
        u = _layer_norm(x_ref[...]) * (1.0 + m[1:2, :]) + m[0:1, :]
        ub = u.astype(BF16)
        u_ref[...] = ub
        dt_ref[...] = _dot(ub, wdt_ref[...])

    o_ref[...] = _dot(u_ref[...], w_ref[...])


def _in_proj(x2, mod3, w_main, w_dt, seq):
    n_tok, d = x2.shape
    tm, tn = 1024, 1024
    tiles_per_batch = seq // tm
    return pl.pallas_call(
        _inproj_kernel,
        grid=(n_tok // tm, MAIN_DIM // tn),
        in_specs=[pl.BlockSpec((tm, d), lambda i, j: (i, 0)),
                  pl.BlockSpec((1, 6, d), lambda i, j: (i // tiles_per_batch, 0, 0)),
                  pl.BlockSpec((d, tn), lambda i, j: (0, j)),
                  pl.BlockSpec((d, DT_PAD), lambda i, j: (0, 0))],
        out_specs=[pl.BlockSpec((tm, tn), lambda i, j: (i, j)),
                   pl.BlockSpec((tm, DT_PAD), lambda i, j: (i, 0))],
        out_shape=[jax.ShapeDtypeStruct((n_tok, MAIN_DIM), F32),
                   jax.ShapeDtypeStruct((n_tok, DT_PAD), F32)],
        scratch_shapes=[pltpu.VMEM((tm, d), BF16)],
        compiler_params=_cparams(("parallel", "arbitrary")),
        name="in_proj",
    )(x2, mod3, w_main, w_dt)


def _hgrn_kernel(q_ref, f_ref, i_ref, g_ref, lb_ref, gn_ref, o_ref,
                 st_ref, qt_ref, qh_ref, kh_ref, kall_ref, vall_ref, dec_ref, oacc_ref,
                 *, n_chunks):
    @pl.when(pl.program_id(1) == 0)
    def _():
        st_ref[...] = jnp.zeros_like(st_ref)

    lbp = lb_ref[...]
    lbe = jnp.exp(lbp - jnp.max(lbp, axis=0, keepdims=True))
    lb = lbe[0:1, :] / jnp.sum(lbe, axis=0, keepdims=True)
    gn = gn_ref[...]
    tril = _tril_bf16(CHUNK)

    def prep(c, carry):
        r0 = c * CHUNK
        s0 = c * STACK
        rows = _ds(r0, CHUNK, CHUNK)
        f = lb + (1.0 - lb) * jax.nn.sigmoid(f_ref[0, rows, :])
        kk = 1.0 - f
        bc = _cumsum_rows(tril, jnp.log(f))
        qf = _silu(q_ref[0, rows, :]) * (A_EXPAND ** -0.5)
        vb = i_ref[0, rows, :].astype(BF16)
        for blk in range(N_SUB):
            lo, hi = blk * SUB, (blk + 1) * SUB
            r = bc[lo - 1:lo, :] if blk > 0 else jnp.zeros((1, bc.shape[1]), F32)
            qt = qf[lo:hi] * jnp.exp(bc[lo:hi] - r)
            sub_rows = _ds(r0 + lo, SUB, SUB)
            qt_ref[sub_rows, :] = qt.astype(BF16)
            qh_ref[sub_rows, :] = (qt * jnp.exp(r)).astype(BF16)
            sec = _ds(s0 + SEC_OFF[blk], hi, SUB)
            kall_ref[sec, :] = (kk[:hi] * jnp.exp(jnp.minimum(r - bc[:hi], EXP_CLAMP))).astype(BF16)
            vall_ref[sec, :] = vb[:hi]
        b_last = bc[CHUNK - 1:CHUNK, :]
        kh_ref[rows, :] = (kk * jnp.exp(b_last - bc)).astype(BF16)
        dec_ref[_ds(c * 8, 8, 8), :] = jnp.broadcast_to(
            jnp.exp(b_last), (8, b_last.shape[1]))
        return carry

    for c in range(n_chunks):
        prep(c, 0)

    trow = lax.broadcasted_iota(jnp.int32, (CHUNK, STACK), 0)
    col = lax.broadcasted_iota(jnp.int32, (CHUNK, STACK), 1)
    valid = None
    for blk in range(N_SUB):
        in_sec = (col >= SEC_OFF[blk]) & (col < SEC_OFF[blk] + SUB * (blk + 1))
        ok = in_sec & (trow >= blk * SUB) & (trow < (blk + 1) * SUB) & (col - SEC_OFF[blk] <= trow)
        valid = ok if valid is None else (valid | ok)

    def scan(c, carry):
        rows = _ds(c * CHUNK, CHUNK, CHUNK)
        srows = _ds(c * STACK, STACK, SUB)
        vrows = _ds(c * STACK + SEC_OFF[-1], CHUNK, SUB)
        drow = pl.ds(c * 8, 1)
        heads = [slice(h * A_EXPAND, (h + 1) * A_EXPAND) for h in range(A_HEADS)]
        sc = [_dot_nt(qt_ref[rows, cs], kall_ref[srows, cs]) for cs in heads]
        st = [st_ref[h] for h in range(A_HEADS)]
        inter = [_dot_nt(qh_ref[rows, cs], st[h].astype(BF16)) for h, cs in enumerate(heads)]
        upd = [_dot_tn(vall_ref[vrows, cs], kh_ref[rows, cs]) for cs in heads]
        out = [_dot(jnp.where(valid, sc[h], 0.0).astype(BF16), vall_ref[srows, cs]) + inter[h]
               for h, cs in enumerate(heads)]
        new_st = [st[h] * dec_ref[drow, cs] + upd[h] for h, cs in enumerate(heads)]
        for h, cs in enumerate(heads):
            oacc_ref[rows, cs] = out[h]
            st_ref[h] = new_st[h]
        return carry

    for c in range(n_chunks):
        scan(c, 0)

    def fin(c, carry):
        rows = _ds(c * CHUNK, CHUNK, CHUNK)
        for h in range(A_HEADS):
            cols = slice(h * A_EXPAND, (h + 1) * A_EXPAND)
            o = oacc_ref[rows, cols]
            y = o * lax.rsqrt(jnp.mean(o * o, axis=-1, keepdims=True) + RMS_EPS) * gn
            o_ref[0, rows, cols] = (y * _silu(g_ref[0, rows, cols])).astype(o_ref.dtype)
        return carry

    for c in range(n_chunks):
        fin(c, 0)


def _hgrn(proj3, hgrn_lb, gnorm, t=256):
    bsz, seq, _ = proj3.shape
    nc = t // CHUNK
    blk = lambda off: pl.BlockSpec((1, t, D_MODEL), lambda b, s, o=off // D_MODEL: (b, s, o))
    return pl.pallas_call(
        functools.partial(_hgrn_kernel, n_chunks=nc),
        grid=(bsz, seq // t),
        in_specs=[blk(OFF_Q), blk(OFF_F), blk(OFF_I), blk(OFF_G),
                  pl.BlockSpec(hgrn_lb.shape, lambda b, s: (0, 0)),
                  pl.BlockSpec((1, A_HEAD_I), lambda b, s: (0, 0))],
        out_specs=pl.BlockSpec((1, t, D_MODEL), lambda b, s: (b, s, 0)),
        out_shape=jax.ShapeDtypeStruct((bsz, seq, D_MODEL), BF16),
        scratch_shapes=[pltpu.VMEM((A_HEADS, A_HEAD_I, A_EXPAND), F32),
                        pltpu.VMEM((t, D_MODEL), BF16),
                        pltpu.VMEM((t, D_MODEL), BF16),
                        pltpu.VMEM((t, D_MODEL), BF16),
                        pltpu.VMEM((nc * STACK, D_MODEL), BF16),
                        pltpu.VMEM((nc * STACK, D_MODEL), BF16),
                        pltpu.VMEM((nc * 8, D_MODEL), F32),
                        pltpu.VMEM((t, D_MODEL), F32)],
        compiler_params=_cparams(("parallel", "arbitrary")),
        name="hgrn",
    )(proj3, proj3, proj3, proj3, hgrn_lb, gnorm.reshape(1, A_HEAD_I))


def _conv_silu(pad_ref, x_ref, w_ref, b_ref, first):
    t = x_ref.shape[1]

    @pl.when(first)
    def _():
        pad_ref[0:8, :] = jnp.zeros((8, pad_ref.shape[1]), F32)

    @pl.when(jnp.logical_not(first))
    def _():
        pad_ref[0:8, :] = pad_ref[t:t + 8, :]

    pad_ref[8:t + 8, :] = x_ref[0]
    acc = b_ref[...] + w_ref[B_CONV - 1:B_CONV, :] * pad_ref[8:t + 8, :]
    for j in range(B_CONV - 1):
        off = 8 - (B_CONV - 1) + j
        acc = acc + w_ref[j:j + 1, :] * pad_ref[off:off + t, :]
    return _silu(acc)


def _ssd_kernel(x_ref, bm_ref, cm_ref, z_ref, dt_ref, wx_ref, wb_ref, wc_ref,
                bx_ref, bb_ref, bc_ref, dtb_ref, alog_ref, dsk_ref, nw_ref,
                o_ref,
                xpad, bpad, cpad, xc_ref, bcv_ref, ccv_ref, xdt_ref, ax_ref,
                st_ref, *, n_chunks):
    grp = pl.program_id(1)
    first = pl.program_id(2) == 0

    @pl.when(first)
    def _():
        st_ref[...] = jnp.zeros_like(st_ref)

    hrow = lax.broadcasted_iota(jnp.int32, (DT_PAD, B_GROUP_W), 0)
    hcol = lax.broadcasted_iota(jnp.int32, (DT_PAD, B_GROUP_W), 1) // B_HEADDIM + grp * B_HG
    expand = (hrow == hcol).astype(BF16)
    exp2 = jnp.concatenate([expand, expand], axis=0)

    def expand_row(ref):
        return _dot(_split_bf16(jnp.broadcast_to(ref[...], (8, DT_PAD))), exp2)[0:1, :]

    xc = _conv_silu(xpad, x_ref, wx_ref, bx_ref, first)
    xc_ref[...] = xc
    bcv_ref[...] = _conv_silu(bpad, bm_ref, wb_ref, bb_ref, first).astype(BF16)
    ccv_ref[...] = _conv_silu(cpad, cm_ref, wc_ref, bc_ref, first).astype(BF16)
    dtv = dt_ref[0] + dtb_ref[...]
    delta = jnp.maximum(dtv, 0.0) + jnp.log1p(jnp.exp(-jnp.abs(dtv)))
    delta_x = _dot(_split_bf16(delta), exp2)
    ax_ref[...] = delta_x * (-jnp.exp(expand_row(alog_ref)))
    xdt_ref[...] = xc * delta_x

    tril = _tril_bf16(CHUNK)
    lane = lax.broadcasted_iota(jnp.int32, (CHUNK, 2 * B_HEADDIM), 1)
    trow = lax.broadcasted_iota(jnp.int32, (CHUNK, 2 * B_HEADDIM), 0)
    left = lane < B_HEADDIM
    spos = jnp.where(left, lane, lane - B_HEADDIM)
    causal2 = trow >= spos
    n_pairs = B_GROUP_W // (2 * B_HEADDIM)
    keep = jnp.tile((trow <= spos).astype(BF16), (1, 2 * n_pairs))
    ones8 = jnp.ones((8, CHUNK), BF16)

    dsk = expand_row(dsk_ref)

    def one_chunk(c, st):
        rows = _ds(c * CHUNK, CHUNK, CHUNK)
        bm = bcv_ref[rows, :]
        cm = ccv_ref[rows, :]
        xdt = xdt_ref[rows, :]
        axs = _split_bf16(ax_ref[rows, :])
        acx2 = _dot(tril, axs)
        acx = acx2[:, :B_GROUP_W] + acx2[:, B_GROUP_W:]
        a_last = acx[CHUNK - 1:CHUNK, :]
        kr2 = _dot(ones8, axs * keep)
        krow = kr2[0:1, :B_GROUP_W] + kr2[0:1, B_GROUP_W:]
        seg = acx - krow

        cb2 = _dot_nt(cm, jnp.concatenate([bm, bm], axis=0))
        inter = _dot(cm, st.astype(BF16)) * jnp.exp(acx)
        xw = (xdt * jnp.exp(a_last - acx)).astype(BF16)
        new_st = st * jnp.exp(a_last) + _dot_tn(bm, xw)
        y_tiles = []
        for p in range(n_pairs):
            cs = slice(p * 2 * B_HEADDIM, (p + 1) * 2 * B_HEADDIM)
            decay = jnp.exp(jnp.where(causal2, seg[:, cs], -jnp.inf))
            m = (cb2 * decay).astype(BF16)
            xp = xdt[:, cs]
            bd = jnp.concatenate([jnp.where(left, xp, 0.0), jnp.where(left, 0.0, xp)], axis=0)
            y_tiles.append(_dot(m, bd.astype(BF16)) + inter[:, cs])
        y = jnp.concatenate(y_tiles, axis=1)
        y = (y + xc_ref[rows, :] * dsk) * _silu(z_ref[0, rows, :].astype(F32))
        y = y * lax.rsqrt(jnp.mean(y * y, axis=-1, keepdims=True) + RMS_EPS) * nw_ref[...]
        return y.astype(o_ref.dtype), new_st

    def body(i, carry):
        st = st_ref[...]
        outs = []
        for k in range(SSD_CHUNKS_PER_STEP):
            y, st = one_chunk(i * SSD_CHUNKS_PER_STEP + k, st)
            outs.append(y)
        for k, y in enumerate(outs):
            r0 = (i * SSD_CHUNKS_PER_STEP + k) * CHUNK
            o_ref[0, pl.ds(pl.multiple_of(r0, CHUNK), CHUNK), :] = y
        st_ref[...] = st
        return carry

    lax.fori_loop(0, n_chunks // SSD_CHUNKS_PER_STEP, body, 0)


def _ssd(proj3, dt3, conv_w, conv_b, dt_bias, a_log, d_skip, w_norm, t=512):
    bsz, seq, _ = proj3.shape
    nc = t // CHUNK
    gw, ns = B_GROUP_W, B_STATE
    pad32 = lambda v: jnp.pad(v.reshape(1, B_HEADS), ((0, 0), (0, DT_PAD - B_HEADS)))
    act = lambda width, off: pl.BlockSpec(
        (1, t, width), lambda b, g, s, o=off // width: (b, s, o + g))
    par = lambda rows, width, off: pl.BlockSpec(
        (rows, width), lambda b, g, s, o=off // width: (0, o + g))
    full = lambda shape: pl.BlockSpec(shape, lambda b, g, s: (0, 0))
    cb2 = conv_b.reshape(1, -1)
    coff_x, coff_b, coff_c = 0, B_INNER, B_INNER + B_GROUPS * B_STATE
    return pl.pallas_call(
        functools.partial(_ssd_kernel, n_chunks=nc),
        grid=(bsz, B_GROUPS, seq // t),
        in_specs=[act(gw, OFF_X), act(ns, OFF_B), act(ns, OFF_C), act(gw, OFF_Z),
                  pl.BlockSpec((1, t, DT_PAD), lambda b, g, s: (b, s, 0)),
                  par(B_CONV, gw, coff_x), par(B_CONV, ns, coff_b), par(B_CONV, ns, coff_c),
                  par(1, gw, coff_x), par(1, ns, coff_b), par(1, ns, coff_c),
                  full((1, DT_PAD)), full((1, DT_PAD)), full((1, DT_PAD)),
                  par(1, gw, 0)],
        out_specs=pl.BlockSpec((1, t, gw), lambda b, g, s: (b, s, g)),
        out_shape=jax.ShapeDtypeStruct((bsz, seq, B_INNER), BF16),
        scratch_shapes=[pltpu.VMEM((t + 8, gw), F32), pltpu.VMEM((t + 8, ns), F32),
                        pltpu.VMEM((t + 8, ns), F32),
                        pltpu.VMEM((t, gw), F32),
                        pltpu.VMEM((t, ns), BF16),
                        pltpu.VMEM((t, ns), BF16),
                        pltpu.VMEM((t, gw), F32),
                        pltpu.VMEM((t, gw), F32),
                        pltpu.VMEM((ns, gw), F32)],
        compiler_params=_cparams(("parallel", "parallel", "arbitrary")),
        name="ssd",
    )(proj3, proj3, proj3, proj3, dt3, conv_w, conv_w, conv_w, cb2, cb2, cb2,
      pad32(dt_bias), pad32(a_log), pad32(d_skip), w_norm.reshape(1, B_INNER))


def _merge_kernel(ya_ref, yb_ref, ga_ref, gb_ref, x_ref, mod_ref, wa_ref, wb_ref, wo_ref,
                  lg_ref, lbias_ref, o_ref):
    m = mod_ref[0]
    pa = _dot(ya_ref[...], wa_ref[...])
    pb = _dot(yb_ref[...], wb_ref[...])
    merged = jax.nn.sigmoid(ga_ref[...]) * pa + jax.nn.sigmoid(gb_ref[...]) * pb
    h = _dot(merged.astype(BF16), wo_ref[...])
    y = _layer_norm(ALPHA * x_ref[...] + m[2:3, :] * h)
    o_ref[...] = y * lg_ref[...] + lbias_ref[...]


def _merge(ya2, yb2, proj2, x2, mod3, w_a, w_b, w_o, ln_g, ln_b, seq):
    n_tok, d = x2.shape
    tm = 512
    tiles_per_batch = seq // tm
    row = lambda width, off=0: pl.BlockSpec((tm, width), lambda i, o=off // width: (i, o))
    full = lambda shape: pl.BlockSpec(shape, lambda i: (0, 0))
    return pl.pallas_call(
        _merge_kernel,
        grid=(n_tok // tm,),
        in_specs=[row(d), row(B_INNER), row(d, OFF_GA), row(d, OFF_GB), row(d),
                  pl.BlockSpec((1, 6, d), lambda i: (i // tiles_per_batch, 0, 0)),
                  full((d, d)), full((B_INNER, d)), full((d, d)),
                  full((1, d)), full((1, d))],
        out_specs=row(d),
        out_shape=jax.ShapeDtypeStruct((n_tok, d), F32),
        compiler_params=_cparams(("parallel",)),
        name="merge",
    )(ya2, yb2, proj2, proj2, x2, mod3, w_a, w_b, w_o, ln_g.reshape(1, d), ln_b.reshape(1, d))


def _ffn_kernel(x_ref, mod_ref, wg_ref, wu_ref, wd_ref, lg_ref, lbias_ref, o_ref,
                u_ref, acc_ref):
    j = pl.program_id(1)
    m = mod_ref[0]

    @pl.when(j == 0)
    def _():
        u = _layer_norm(x_ref[...]) * (1.0 + m[4:5, :]) + m[3:4, :]
        u_ref[...] = u.astype(BF16)
        acc_ref[...] = jnp.zeros_like(acc_ref)

    u = u_ref[...]
    hidden = _silu(_dot(u, wg_ref[...])) * _dot(u, wu_ref[...])
    acc_ref[...] += _dot(hidden.astype(BF16), wd_ref[...])

    @pl.when(j == pl.num_programs(1) - 1)
    def _():
        y = _layer_norm(ALPHA * x_ref[...] + m[5:6, :] * acc_ref[...])
        o_ref[...] = y * lg_ref[...] + lbias_ref[...]


def _ffn(x2, mod3, w_gate, w_up, w_down, ln_g, ln_b, seq):
    n_tok, d = x2.shape
    tm, tf = 1024, D_FF // 2
    tiles_per_batch = seq // tm
    full = lambda shape: pl.BlockSpec(shape, lambda i, j: (0, 0))
    return pl.pallas_call(
        _ffn_kernel,
        grid=(n_tok // tm, D_FF // tf),
        in_specs=[pl.BlockSpec((tm, d), lambda i, j: (i, 0)),
                  pl.BlockSpec((1, 6, d), lambda i, j: (i // tiles_per_batch, 0, 0)),
                  pl.BlockSpec((d, tf), lambda i, j: (0, j)),
                  pl.BlockSpec((d, tf), lambda i, j: (0, j)),
                  pl.BlockSpec((tf, d), lambda i, j: (j, 0)),
                  full((1, d)), full((1, d))],
        out_specs=pl.BlockSpec((tm, d), lambda i, j: (i, 0)),
        out_shape=jax.ShapeDtypeStruct((n_tok, d), F32),
        scratch_shapes=[pltpu.VMEM((tm, d), BF16), pltpu.VMEM((tm, d), F32)],
        compiler_params=_cparams(("parallel", "arbitrary")),
        name="ffn",
    )(x2, mod3, w_gate, w_up, w_down, ln_g.reshape(1, d), ln_b.reshape(1, d))


def kernel(x, c, w_ada, b_ada, w_in, hgrn_lb, hgrn_gnorm, ssm_conv_w, ssm_conv_b, ssm_dt_bias,
           ssm_a_log, ssm_d, ssm_norm, w_branch_a, w_branch_b, w_o, ln1_g, ln1_b,
           w_ffn_gate, w_ffn_up, w_ffn_down, ln2_g, ln2_b):
    bsz, seq, d = x.shape
    assert d == D_MODEL and w_ada.shape[0] == DEPTH == 1
    x2 = x.reshape(bsz * seq, d)
    for l in range(DEPTH):
        mod3 = _ada(c, w_ada[l], b_ada[l]).reshape(bsz, 6, d)
        w = w_in[l]
        dt_lo = OFF_X + B_INNER + 2 * B_GROUPS * B_STATE
        w_main = jnp.concatenate([w[:, :dt_lo], w[:, dt_lo + B_HEADS:]], axis=1).astype(BF16)
        w_dt = jnp.pad(w[:, dt_lo:dt_lo + B_HEADS], ((0, 0), (0, DT_PAD - B_HEADS))).astype(BF16)
        proj2, dt2 = _in_proj(x2, mod3, w_main, w_dt, seq)
        proj3 = proj2.reshape(bsz, seq, MAIN_DIM)
        ya = _hgrn(proj3, hgrn_lb, hgrn_gnorm[l])
        yb = _ssd(proj3, dt2.reshape(bsz, seq, DT_PAD), ssm_conv_w[l], ssm_conv_b[l],
                  ssm_dt_bias[l], ssm_a_log[l], ssm_d[l], ssm_norm[l])
        x2 = _merge(ya.reshape(bsz * seq, d), yb.reshape(bsz * seq, B_INNER), proj2, x2, mod3,
                    w_branch_a[l].astype(BF16), w_branch_b[l].astype(BF16), w_o[l].astype(BF16),
                    ln1_g[l], ln1_b[l], seq)
        x2 = _ffn(x2, mod3, w_ffn_gate[l].astype(BF16), w_ffn_up[l].astype(BF16),
                  w_ffn_down[l].astype(BF16), ln2_g[l], ln2_b[l], seq)
    return x2.reshape(bsz, seq, d)
```

```python
import functools

import jax
import jax.numpy as jnp
from jax import lax
from jax.experimental import pallas as pl
from jax.experimental.pallas import tpu as pltpu

F32 = jnp.float32
BF16 = jnp.bfloat16

D_MODEL = 1024
DEPTH = 1
A_HEADS = 8
A_EXPAND = 128
A_HEAD_I = 128
B_INNER = 2048
B_HEADDIM = 64
B_HEADS = 32
B_GROUPS = 4
B_HG = 8
B_STATE = 128
B_CONV = 4
B_GROUP_W = B_INNER // B_GROUPS
D_FF = 2816
ALPHA = (2.0 * DEPTH) ** 0.25
LN_EPS = 1e-5
RMS_EPS = 1e-6

OFF_Q, OFF_F, OFF_I, OFF_G = 0, 1024, 2048, 3072
OFF_Z = 4096
OFF_X = 6144
OFF_B = OFF_X + B_INNER
OFF_C = OFF_B + B_GROUPS * B_STATE
OFF_GA = 9216
OFF_GB = 10240
MAIN_DIM = 11264
DT_PAD = 128

CHUNK = 64
SUB = 16
N_SUB = CHUNK // SUB
SEC_OFF = tuple(SUB * i * (i + 1) // 2 for i in range(N_SUB))
STACK = SUB * N_SUB * (N_SUB + 1) // 2
EXP_CLAMP = 80.0
SSD_CHUNKS_PER_STEP = 8

VMEM_LIMIT = 56 * 1024 * 1024


def _cparams(sem):
    return pltpu.CompilerParams(dimension_semantics=sem, vmem_limit_bytes=VMEM_LIMIT)


def _layer_norm(x):
    mu = jnp.mean(x, axis=-1, keepdims=True)
    xc = x - mu
    return xc * lax.rsqrt(jnp.mean(xc * xc, axis=-1, keepdims=True) + LN_EPS)


def _silu(x):
    return x * jax.nn.sigmoid(x)


def _split_bf16(x, axis=-1):
    hi = x.astype(BF16)
    lo = (x - hi.astype(F32)).astype(BF16)
    return jnp.concatenate([hi, lo], axis=axis)


def _dot(a, b):
    return jnp.dot(a, b, preferred_element_type=F32)


def _dot_nt(a, b):
    return lax.dot_general(a, b, (((1,), (1,)), ((), ())), preferred_element_type=F32)


def _dot_tn(a, b):
    return lax.dot_general(a, b, (((0,), (0,)), ((), ())), preferred_element_type=F32)


def _ds(start, size, align):
    if isinstance(start, int):
        return pl.ds(start, size)
    return pl.ds(pl.multiple_of(start, align), size)


def _tril_bf16(n):
    return (lax.broadcasted_iota(jnp.int32, (n, n), 0)
            >= lax.broadcasted_iota(jnp.int32, (n, n), 1)).astype(BF16)


def _cumsum_rows(tril, x):
    w = x.shape[1]
    y = _dot(tril, _split_bf16(x))
    return y[:, :w] + y[:, w:]


def _ada_kernel(c_ref, w_ref, b_ref, o_ref):
    cond = _silu(c_ref[...])
    o_ref[...] = jnp.dot(cond, w_ref[...], preferred_element_type=F32,
                         precision=lax.Precision.HIGHEST) + b_ref[...]


def _ada(c, w_ada, b_ada):
    bsz, d = c.shape
    n = w_ada.shape[1]
    tn = 1536
    return pl.pallas_call(
        _ada_kernel,
        grid=(n // tn,),
        in_specs=[pl.BlockSpec((bsz, d), lambda j: (0, 0)),
                  pl.BlockSpec((d, tn), lambda j: (0, j)),
                  pl.BlockSpec((1, tn), lambda j: (0, j))],
        out_specs=pl.BlockSpec((bsz, tn), lambda j: (0, j)),
        out_shape=jax.ShapeDtypeStruct((bsz, n), F32),
        compiler_params=_cparams(("arbitrary",)),
        name="ada",
    )(c, w_ada, b_ada.reshape(1, n))


TILE_Q, TILE_F, TILE_G, TILE_X0, TILE_GATE0 = 0, 1, 3, 6, 9
INPROJ_ROWS = 256


def _inproj_tile(j):
    return jnp.where(j < 2, 1 - j, j)


def _inproj_kernel(x_ref, mod_ref, w_ref, wdt_ref, lb_ref, o_ref, lg_ref, dt_ref, u_ref):
    j = pl.program_id(1)
    tm = x_ref.shape[0]
    pieces = [slice(r, r + INPROJ_ROWS) for r in range(0, tm, INPROJ_ROWS)]

    @pl.when(j == 0)
    def _():
        m = mod_ref[0]
        lbp = lb_ref[...]
        lbe = jnp.exp(lbp - jnp.max(lbp, axis=0, keepdims=True))
        lb = lbe[0:1, :] / jnp.sum(lbe, axis=0, keepdims=True)
        for rs in pieces:
            u = _layer_norm(x_ref[rs, :]) * (1.0 + m[1:2, :]) + m[0:1, :]
            ub = u.astype(BF16)
            u_ref[rs, :] = ub
            dt_ref[rs, :] = _dot(ub, wdt_ref[...])
            f = lb + (1.0 - lb) * jax.nn.sigmoid(_dot(ub, w_ref[...]))
            lg_ref[rs, :] = jnp.log(f)
            o_ref[rs, :] = (1.0 - f).astype(o_ref.dtype)

    t = _inproj_tile(j)
    is_q = t == TILE_Q
    is_silu = is_q | ((t >= TILE_G) & (t < TILE_X0))
    is_gate = t >= TILE_GATE0

    @pl.when(is_silu)
    def _():
        scale = jnp.where(is_q, A_EXPAND ** -0.5, 1.0)
        for rs in pieces:
            acc = _dot(u_ref[rs, :], w_ref[...])
            o_ref[rs, :] = (acc * (jax.nn.sigmoid(acc) * scale)).astype(o_ref.dtype)

    @pl.when(is_gate)
    def _():
        for rs in pieces:
            o_ref[rs, :] = jax.nn.sigmoid(_dot(u_ref[rs, :], w_ref[...])).astype(o_ref.dtype)

    @pl.when((j > 0) & jnp.logical_not(is_silu | is_gate))
    def _():
        o_ref[...] = _dot(u_ref[...], w_ref[...]).astype(o_ref.dtype)


def _in_proj(x2, mod3, w_main, w_dt, hgrn_lb, seq):
    n_tok, d = x2.shape
    tm, tn = 1024, D_MODEL
    tiles_per_batch = seq // tm
    return pl.pallas_call(
        _inproj_kernel,
        grid=(n_tok // tm, MAIN_DIM // tn),
        in_specs=[pl.BlockSpec((tm, d), lambda i, j: (i, 0)),
                  pl.BlockSpec((1, 6, d), lambda i, j: (i // tiles_per_batch, 0, 0)),
                  pl.BlockSpec((d, tn), lambda i, j: (0, _inproj_tile(j))),
                  pl.BlockSpec((d, DT_PAD), lambda i, j: (0, 0)),
                  pl.BlockSpec(hgrn_lb.shape, lambda i, j: (0, 0))],
        out_specs=[pl.BlockSpec((tm, tn), lambda i, j: (i, _inproj_tile(j))),
                   pl.BlockSpec((tm, tn), lambda i, j: (i, 0)),
                   pl.BlockSpec((tm, DT_PAD), lambda i, j: (i, 0))],
        out_shape=[jax.ShapeDtypeStruct((n_tok, MAIN_DIM), BF16),
                   jax.ShapeDtypeStruct((n_tok, tn), F32),
                   jax.ShapeDtypeStruct((n_tok, DT_PAD), F32)],
        scratch_shapes=[pltpu.VMEM((tm, d), BF16)],
        compiler_params=_cparams(("parallel", "arbitrary")),
        name="in_proj",
    )(x2, mod3, w_main, w_dt, hgrn_lb)


def _hgrn_kernel(q_ref, k_ref, i_ref, g_ref, lg_ref, gn_ref, o_ref,
                 st_ref, qt_ref, qh_ref, kh_ref, kall_ref, vall_ref, dec_ref, oacc_ref,
                 *, n_chunks):
    @pl.when(pl.program_id(1) == 0)
    def _():
        st_ref[...] = jnp.zeros_like(st_ref)

    gn = gn_ref[...]
    tril = _tril_bf16(CHUNK)

    def prep(c, carry):
        r0 = c * CHUNK
        s0 = c * STACK
        rows = _ds(r0, CHUNK, CHUNK)
        kk = k_ref[0, rows, :].astype(F32)
        bc = _cumsum_rows(tril, lg_ref[0, rows, :])
        qf = q_ref[0, rows, :].astype(F32)
        vb = i_ref[0, rows, :]
        for blk in range(N_SUB):
            lo, hi = blk * SUB, (blk + 1) * SUB
            r = bc[lo - 1:lo, :] if blk > 0 else jnp.zeros((1, bc.shape[1]), F32)
            qt = qf[lo:hi] * jnp.exp(bc[lo:hi] - r)
            sub_rows = _ds(r0 + lo, SUB, SUB)
            qt_ref[sub_rows, :] = qt.astype(BF16)
            qh_ref[sub_rows, :] = (qt * jnp.exp(r)).astype(BF16)
            sec = _ds(s0 + SEC_OFF[blk], hi, SUB)
            kall_ref[sec, :] = (kk[:hi] * jnp.exp(jnp.minimum(r - bc[:hi], EXP_CLAMP))).astype(BF16)
            vall_ref[sec, :] = vb[:hi]
        b_last = bc[CHUNK - 1:CHUNK, :]
        kh_ref[rows, :] = (kk * jnp.exp(b_last - bc)).astype(BF16)
        dec_ref[_ds(c * 8, 8, 8), :] = jnp.broadcast_to(
            jnp.exp(b_last), (8, b_last.shape[1]))
        return carry

    for c in range(n_chunks):
        prep(c, 0)

    trow = lax.broadcasted_iota(jnp.int32, (CHUNK, STACK), 0)
    col = lax.broadcasted_iota(jnp.int32, (CHUNK, STACK), 1)
    valid = None
    for blk in range(N_SUB):
        in_sec = (col >= SEC_OFF[blk]) & (col < SEC_OFF[blk] + SUB * (blk + 1))
        ok = in_sec & (trow >= blk * SUB) & (trow < (blk + 1) * SUB) & (col - SEC_OFF[blk] <= trow)
        valid = ok if valid is None else (valid | ok)

    def scan(c, carry):
        rows = _ds(c * CHUNK, CHUNK, CHUNK)
        srows = _ds(c * STACK, STACK, SUB)
        vrows = _ds(c * STACK + SEC_OFF[-1], CHUNK, SUB)
        drow = pl.ds(c * 8, 1)
        heads = [slice(h * A_EXPAND, (h + 1) * A_EXPAND) for h in range(A_HEADS)]
        sc = [_dot_nt(qt_ref[rows, cs], kall_ref[srows, cs]) for cs in heads]
        st = [st_ref[h] for h in range(A_HEADS)]
        inter = [_dot_nt(qh_ref[rows, cs], st[h].astype(BF16)) for h, cs in enumerate(heads)]
        upd = [_dot_tn(vall_ref[vrows, cs], kh_ref[rows, cs]) for cs in heads]
        out = [_dot(jnp.where(valid, sc[h], 0.0).astype(BF16), vall_ref[srows, cs]) + inter[h]
               for h, cs in enumerate(heads)]
        new_st = [st[h] * dec_ref[drow, cs] + upd[h] for h, cs in enumerate(heads)]
        for h, cs in enumerate(heads):
            oacc_ref[rows, cs] = out[h]
            st_ref[h] = new_st[h]
        return carry

    for c in range(n_chunks):
        scan(c, 0)

    def fin(c, carry):
        rows = _ds(c * CHUNK, CHUNK, CHUNK)
        for h in range(A_HEADS):
            cols = slice(h * A_EXPAND, (h + 1) * A_EXPAND)
            o = oacc_ref[rows, cols]
            y = o * lax.rsqrt(jnp.mean(o * o, axis=-1, keepdims=True) + RMS_EPS) * gn
            o_ref[0, rows, cols] = (y * g_ref[0, rows, cols].astype(F32)).astype(o_ref.dtype)
        return carry

    for c in range(n_chunks):
        fin(c, 0)


def _hgrn(proj3, lg3, gnorm, t=256):
    bsz, seq, _ = proj3.shape
    nc = t // CHUNK
    blk = lambda off: pl.BlockSpec((1, t, D_MODEL), lambda b, s, o=off // D_MODEL: (b, s, o))
    return pl.pallas_call(
        functools.partial(_hgrn_kernel, n_chunks=nc),
        grid=(bsz, seq // t),
        in_specs=[blk(OFF_Q), blk(OFF_F), blk(OFF_I), blk(OFF_G), blk(0),
                  pl.BlockSpec((1, A_HEAD_I), lambda b, s: (0, 0))],
        out_specs=pl.BlockSpec((1, t, D_MODEL), lambda b, s: (b, s, 0)),
        out_shape=jax.ShapeDtypeStruct((bsz, seq, D_MODEL), BF16),
        scratch_shapes=[pltpu.VMEM((A_HEADS, A_HEAD_I, A_EXPAND), F32),
                        pltpu.VMEM((t, D_MODEL), BF16),
                        pltpu.VMEM((t, D_MODEL), BF16),
                        pltpu.VMEM((t, D_MODEL), BF16),
                        pltpu.VMEM((nc * STACK, D_MODEL), BF16),
                        pltpu.VMEM((nc * STACK, D_MODEL), BF16),
                        pltpu.VMEM((nc * 8, D_MODEL), F32),
                        pltpu.VMEM((t, D_MODEL), F32)],
        compiler_params=_cparams(("parallel", "arbitrary")),
        name="hgrn",
    )(proj3, proj3, proj3, proj3, lg3, gnorm.reshape(1, A_HEAD_I))


def _conv_silu(pad_ref, x_ref, w_ref, b_ref, first):
    t = x_ref.shape[1]

    @pl.when(first)
    def _():
        pad_ref[0:8, :] = jnp.zeros((8, pad_ref.shape[1]), F32)

    @pl.when(jnp.logical_not(first))
    def _():
        pad_ref[0:8, :] = pad_ref[t:t + 8, :]

    pad_ref[8:t + 8, :] = x_ref[0].astype(F32)
    acc = b_ref[...] + w_ref[B_CONV - 1:B_CONV, :] * pad_ref[8:t + 8, :]
    for j in range(B_CONV - 1):
        off = 8 - (B_CONV - 1) + j
        acc = acc + w_ref[j:j + 1, :] * pad_ref[off:off + t, :]
    return _silu(acc)


def _ssd_kernel(x_ref, bm_ref, cm_ref, z_ref, dt_ref, wx_ref, wb_ref, wc_ref,
                bx_ref, bb_ref, bc_ref, dtb_ref, alog_ref, dsk_ref, nw_ref,
                o_ref,
                xpad, bpad, cpad, xc_ref, bcv_ref, ccv_ref, xdt_ref, ax_ref,
                st_ref, *, n_chunks):
    grp = pl.program_id(1)
    first = pl.program_id(2) == 0

    @pl.when(first)
    def _():
        st_ref[...] = jnp.zeros_like(st_ref)

    hrow = lax.broadcasted_iota(jnp.int32, (DT_PAD, B_GROUP_W), 0)
    hcol = lax.broadcasted_iota(jnp.int32, (DT_PAD, B_GROUP_W), 1) // B_HEADDIM + grp * B_HG
    expand = (hrow == hcol).astype(BF16)
    exp2 = jnp.concatenate([expand, expand], axis=0)

    def expand_row(ref):
        return _dot(_split_bf16(jnp.broadcast_to(ref[...], (8, DT_PAD))), exp2)[0:1, :]

    xc = _conv_silu(xpad, x_ref, wx_ref, bx_ref, first)
    xc_ref[...] = xc
    bcv_ref[...] = _conv_silu(bpad, bm_ref, wb_ref, bb_ref, first).astype(BF16)
    ccv_ref[...] = _conv_silu(cpad, cm_ref, wc_ref, bc_ref, first).astype(BF16)
    dtv = dt_ref[0] + dtb_ref[...]
    delta = jnp.maximum(dtv, 0.0) + jnp.log1p(jnp.exp(-jnp.abs(dtv)))
    delta_x = _dot(_split_bf16(delta), exp2)
    ax_ref[...] = delta_x * (-jnp.exp(expand_row(alog_ref)))
    xdt_ref[...] = xc * delta_x

    tril = _tril_bf16(CHUNK)
    lane = lax.broadcasted_iota(jnp.int32, (CHUNK, 2 * B_HEADDIM), 1)
    trow = lax.broadcasted_iota(jnp.int32, (CHUNK, 2 * B_HEADDIM), 0)
    left = lane < B_HEADDIM
    spos = jnp.where(left, lane, lane - B_HEADDIM)
    causal2 = trow >= spos
    n_pairs = B_GROUP_W // (2 * B_HEADDIM)
    keep = jnp.tile((trow <= spos).astype(BF16), (1, 2 * n_pairs))
    ones8 = jnp.ones((8, CHUNK), BF16)

    dsk = expand_row(dsk_ref)

    def one_chunk(c, st):
        rows = _ds(c * CHUNK, CHUNK, CHUNK)
        bm = bcv_ref[rows, :]
        cm = ccv_ref[rows, :]
        xdt = xdt_ref[rows, :]
        axs = _split_bf16(ax_ref[rows, :])
        acx2 = _dot(tril, axs)
        acx = acx2[:, :B_GROUP_W] + acx2[:, B_GROUP_W:]
        a_last = acx[CHUNK - 1:CHUNK, :]
        kr2 = _dot(ones8, axs * keep)
        krow = kr2[0:1, :B_GROUP_W] + kr2[0:1, B_GROUP_W:]
        seg = acx - krow

        cb2 = _dot_nt(cm, jnp.concatenate([bm, bm], axis=0))
        inter = _dot(cm, st.astype(BF16)) * jnp.exp(acx)
        xw = (xdt * jnp.exp(a_last - acx)).astype(BF16)
        new_st = st * jnp.exp(a_last) + _dot_tn(bm, xw)
        y_tiles = []
        for p in range(n_pairs):
            cs = slice(p * 2 * B_HEADDIM, (p + 1) * 2 * B_HEADDIM)
            decay = jnp.exp(jnp.where(causal2, seg[:, cs], -jnp.inf))
            m = (cb2 * decay).astype(BF16)
            xp = xdt[:, cs]
            bd = jnp.concatenate([jnp.where(left, xp, 0.0), jnp.where(left, 0.0, xp)], axis=0)
            y_tiles.append(_dot(m, bd.astype(BF16)) + inter[:, cs])
        y = jnp.concatenate(y_tiles, axis=1)
        y = (y + xc_ref[rows, :] * dsk) * z_ref[0, rows, :].astype(F32)
        y = y * lax.rsqrt(jnp.mean(y * y, axis=-1, keepdims=True) + RMS_EPS) * nw_ref[...]
        return y.astype(o_ref.dtype), new_st

    def body(i, carry):
        st = st_ref[...]
        outs = []
        for k in range(SSD_CHUNKS_PER_STEP):
            y, st = one_chunk(i * SSD_CHUNKS_PER_STEP + k, st)
            outs.append(y)
        for k, y in enumerate(outs):
            r0 = (i * SSD_CHUNKS_PER_STEP + k) * CHUNK
            o_ref[0, pl.ds(pl.multiple_of(r0, CHUNK), CHUNK), :] = y
        st_ref[...] = st
        return carry

    lax.fori_loop(0, n_chunks // SSD_CHUNKS_PER_STEP, body, 0)


def _ssd(proj3, dt3, conv_w, conv_b, dt_bias, a_log, d_skip, w_norm, t=512):
    bsz, seq, _ = proj3.shape
    nc = t // CHUNK
    gw, ns = B_GROUP_W, B_STATE
    pad32 = lambda v: jnp.pad(v.reshape(1, B_HEADS), ((0, 0), (0, DT_PAD - B_HEADS)))
    act = lambda width, off: pl.BlockSpec(
        (1, t, width), lambda b, g, s, o=off // width: (b, s, o + g))
    par = lambda rows, width, off: pl.BlockSpec(
        (rows, width), lambda b, g, s, o=off // width: (0, o + g))
    full = lambda shape: pl.BlockSpec(shape, lambda b, g, s: (0, 0))
    cb2 = conv_b.reshape(1, -1)
    coff_x, coff_b, coff_c = 0, B_INNER, B_INNER + B_GROUPS * B_STATE
    return pl.pallas_call(
        functools.partial(_ssd_kernel, n_chunks=nc),
        grid=(bsz, B_GROUPS, seq // t),
        in_specs=[act(gw, OFF_X), act(ns, OFF_B), act(ns, OFF_C), act(gw, OFF_Z),
                  pl.BlockSpec((1, t, DT_PAD), lambda b, g, s: (b, s, 0)),
                  par(B_CONV, gw, coff_x), par(B_CONV, ns, coff_b), par(B_CONV, ns, coff_c),
                  par(1, gw, coff_x), par(1, ns, coff_b), par(1, ns, coff_c),
                  full((1, DT_PAD)), full((1, DT_PAD)), full((1, DT_PAD)),
                  par(1, gw, 0)],
        out_specs=pl.BlockSpec((1, t, gw), lambda b, g, s: (b, s, g)),
        out_shape=jax.ShapeDtypeStruct((bsz, seq, B_INNER), BF16),
        scratch_shapes=[pltpu.VMEM((t + 8, gw), F32), pltpu.VMEM((t + 8, ns), F32),
                        pltpu.VMEM((t + 8, ns), F32),
                        pltpu.VMEM((t, gw), F32),
                        pltpu.VMEM((t, ns), BF16),
                        pltpu.VMEM((t, ns), BF16),
                        pltpu.VMEM((t, gw), F32),
                        pltpu.VMEM((t, gw), F32),
                        pltpu.VMEM((ns, gw), F32)],
        compiler_params=_cparams(("parallel", "parallel", "arbitrary")),
        name="ssd",
    )(proj3, proj3, proj3, proj3, dt3, conv_w, conv_w, conv_w, cb2, cb2, cb2,
      pad32(dt_bias), pad32(a_log), pad32(d_skip), w_norm.reshape(1, B_INNER))


def _merge_kernel(ya_ref, yb_ref, ga_ref, gb_ref, x_ref, mod_ref, wa_ref, wb_ref, wo_ref,
                  lg_ref, lbias_ref, o_ref):
    m = mod_ref[0]
    for r in range(0, x_ref.shape[0], MERGE_ROWS):
        rs = slice(r, r + MERGE_ROWS)
        pa = _dot(ya_ref[rs, :], wa_ref[...])
        pb = _dot(yb_ref[rs, :], wb_ref[...])
        merged = ga_ref[rs, :].astype(F32) * pa + gb_ref[rs, :].astype(F32) * pb
        h = _dot(merged.astype(BF16), wo_ref[...])
        y = _layer_norm(ALPHA * x_ref[rs, :] + m[2:3, :] * h)
        o_ref[rs, :] = y * lg_ref[...] + lbias_ref[...]


MERGE_ROWS = 256


def _merge(ya2, yb2, proj2, x2, mod3, w_a, w_b, w_o, ln_g, ln_b, seq):
    n_tok, d = x2.shape
    tm = 1024
    tiles_per_batch = seq // tm
    row = lambda width, off=0: pl.BlockSpec((tm, width), lambda i, o=off // width: (i, o))
    full = lambda shape: pl.BlockSpec(shape, lambda i: (0, 0), pipeline_mode=pl.Buffered(1))
    return pl.pallas_call(
        _merge_kernel,
        grid=(n_tok // tm,),
        in_specs=[row(d), row(B_INNER), row(d, OFF_GA), row(d, OFF_GB), row(d),
                  pl.BlockSpec((1, 6, d), lambda i: (i // tiles_per_batch, 0, 0)),
                  full((d, d)), full((B_INNER, d)), full((d, d)),
                  full((1, d)), full((1, d))],
        out_specs=row(d),
        out_shape=jax.ShapeDtypeStruct((n_tok, d), F32),
        compiler_params=_cparams(("parallel",)),
        name="merge",
    )(ya2, yb2, proj2, proj2, x2, mod3, w_a, w_b, w_o, ln_g.reshape(1, d), ln_b.reshape(1, d))


FFN_ROWS = 256


def _ffn_kernel(x_ref, mod_ref, wg_ref, wu_ref, wd_ref, lg_ref, lbias_ref, o_ref):
    m = mod_ref[0]
    for r in range(0, x_ref.shape[0], FFN_ROWS):
        rs = slice(r, r + FFN_ROWS)
        x = x_ref[rs, :]
        u = (_layer_norm(x) * (1.0 + m[4:5, :]) + m[3:4, :]).astype(BF16)
        hidden = _silu(_dot(u, wg_ref[...])) * _dot(u, wu_ref[...])
        h = _dot(hidden.astype(BF16), wd_ref[...])
        y = _layer_norm(ALPHA * x + m[5:6, :] * h)
        o_ref[rs, :] = y * lg_ref[...] + lbias_ref[...]


def _ffn(x2, mod3, w_gate, w_up, w_down, ln_g, ln_b, seq):
    n_tok, d = x2.shape
    tm = 1024
    tiles_per_batch = seq // tm
    once = lambda shape: pl.BlockSpec(shape, lambda i: (0, 0), pipeline_mode=pl.Buffered(1))
    return pl.pallas_call(
        _ffn_kernel,
        grid=(n_tok // tm,),
        in_specs=[pl.BlockSpec((tm, d), lambda i: (i, 0)),
                  pl.BlockSpec((1, 6, d), lambda i: (i // tiles_per_batch, 0, 0)),
                  once((d, D_FF)), once((d, D_FF)), once((D_FF, d)),
                  once((1, d)), once((1, d))],
        out_specs=pl.BlockSpec((tm, d), lambda i: (i, 0)),
        out_shape=jax.ShapeDtypeStruct((n_tok, d), F32),
        compiler_params=_cparams(("parallel",)),
        name="ffn",
    )(x2, mod3, w_gate, w_up, w_down, ln_g.reshape(1, d), ln_b.reshape(1, d))


def kernel(x, c, w_ada, b_ada, w_in, hgrn_lb, hgrn_gnorm, ssm_conv_w, ssm_conv_b, ssm_dt_bias,
           ssm_a_log, ssm_d, ssm_norm, w_branch_a, w_branch_b, w_o, ln1_g, ln1_b,
           w_ffn_gate, w_ffn_up, w_ffn_down, ln2_g, ln2_b):
    bsz, seq, d = x.shape
    assert d == D_MODEL and w_ada.shape[0] == DEPTH == 1
    x2 = x.reshape(bsz * seq, d)
    for l in range(DEPTH):
        mod3 = _ada(c, w_ada[l], b_ada[l]).reshape(bsz, 6, d)
        w = w_in[l]
        dt_lo = OFF_X + B_INNER + 2 * B_GROUPS * B_STATE
        w_main = jnp.concatenate([w[:, :dt_lo], w[:, dt_lo + B_HEADS:]], axis=1).astype(BF16)
        w_dt = jnp.pad(w[:, dt_lo:dt_lo + B_HEADS], ((0, 0), (0, DT_PAD - B_HEADS))).astype(BF16)
        proj2, lg2, dt2 = _in_proj(x2, mod3, w_main, w_dt, hgrn_lb, seq)
        proj3 = proj2.reshape(bsz, seq, MAIN_DIM)
        ya = _hgrn(proj3, lg2.reshape(bsz, seq, D_MODEL), hgrn_gnorm[l])
        yb = _ssd(proj3, dt2.reshape(bsz, seq, DT_PAD), ssm_conv_w[l], ssm_conv_b[l],
                  ssm_dt_bias[l], ssm_a_log[l], ssm_d[l], ssm_norm[l])
        x2 = _merge(ya.reshape(bsz * seq, d), yb.reshape(bsz * seq, B_INNER), proj2, x2, mod3,
                    w_branch_a[l].astype(BF16), w_branch_b[l].astype(BF16), w_o[l].astype(BF16),
                    ln1_g[l], ln1_b[l], seq)
        x2 = _ffn(x2, mod3, w_ffn_gate[l].astype(BF16), w_ffn_up[l].astype(BF16),
                  w_ffn_down[l].astype(BF16), ln2_g[l], ln2_b[l], seq)
    return x2.reshape(bsz, seq, d)
```

```python
import functools

import jax
import jax.numpy as jnp
from jax import lax
from jax.experimental import pallas as pl
from jax.experimental.pallas import tpu as pltpu

F32 = jnp.float32
BF16 = jnp.bfloat16

D_MODEL = 1024
DEPTH = 1
A_HEADS = 8
A_EXPAND = 128
A_HEAD_I = 128
B_INNER = 2048
B_HEADDIM = 64
B_HEADS = 32
B_GROUPS = 4
B_HG = 8
B_STATE = 128
B_CONV = 4
B_GROUP_W = B_INNER // B_GROUPS
D_FF = 2816
ALPHA = (2.0 * DEPTH) ** 0.25
LN_EPS = 1e-5
RMS_EPS = 1e-6

OFF_Q, OFF_F, OFF_I, OFF_G = 0, 1024, 2048, 3072
OFF_Z = 4096
OFF_X = 6144
OFF_B = OFF_X + B_INNER
OFF_C = OFF_B + B_GROUPS * B_STATE
OFF_GA = 9216
OFF_GB = 10240
MAIN_DIM = 11264
DT_PAD = 128

CHUNK = 64
SUB = 16
N_SUB = CHUNK // SUB
SEC_OFF = tuple(SUB * i * (i + 1) // 2 for i in range(N_SUB))
STACK = SUB * N_SUB * (N_SUB + 1) // 2
EXP_CLAMP = 80.0
SSD_CHUNKS_PER_STEP = 8

VMEM_LIMIT = 56 * 1024 * 1024


def _cparams(sem):
    return pltpu.CompilerParams(dimension_semantics=sem, vmem_limit_bytes=VMEM_LIMIT)


def _layer_norm(x):
    mu = jnp.mean(x, axis=-1, keepdims=True)
    xc = x - mu
    return xc * lax.rsqrt(jnp.mean(xc * xc, axis=-1, keepdims=True) + LN_EPS)


def _silu(x):
    return x * jax.nn.sigmoid(x)


def _split_bf16(x, axis=-1):
    hi = x.astype(BF16)
    lo = (x - hi.astype(F32)).astype(BF16)
    return jnp.concatenate([hi, lo], axis=axis)


def _dot(a, b):
    return jnp.dot(a, b, preferred_element_type=F32)


def _dot_nt(a, b):
    return lax.dot_general(a, b, (((1,), (1,)), ((), ())), preferred_element_type=F32)


def _dot_tn(a, b):
    return lax.dot_general(a, b, (((0,), (0,)), ((), ())), preferred_element_type=F32)


def _ds(start, size, align):
    if isinstance(start, int):
        return pl.ds(start, size)
    return pl.ds(pl.multiple_of(start, align), size)


def _tril_bf16(n):
    return (lax.broadcasted_iota(jnp.int32, (n, n), 0)
            >= lax.broadcasted_iota(jnp.int32, (n, n), 1)).astype(BF16)


def _cumsum_rows(tril, x):
    w = x.shape[1]
    y = _dot(tril, _split_bf16(x))
    return y[:, :w] + y[:, w:]


def _ada_kernel(c_ref, w_ref, b_ref, o_ref):
    cond = _silu(c_ref[...])
    o_ref[...] = jnp.dot(cond, w_ref[...], preferred_element_type=F32,
                         precision=lax.Precision.HIGHEST) + b_ref[...]


def _ada(c, w_ada, b_ada):
    bsz, d = c.shape
    n = w_ada.shape[1]
    tn = 1536
    return pl.pallas_call(
        _ada_kernel,
        grid=(n // tn,),
        in_specs=[pl.BlockSpec((bsz, d), lambda j: (0, 0)),
                  pl.BlockSpec((d, tn), lambda j: (0, j)),
                  pl.BlockSpec((1, tn), lambda j: (0, j))],
        out_specs=pl.BlockSpec((bsz, tn), lambda j: (0, j)),
        out_shape=jax.ShapeDtypeStruct((bsz, n), F32),
        compiler_params=_cparams(("arbitrary",)),
        name="ada",
    )(c, w_ada, b_ada.reshape(1, n))


TILE_Q, TILE_F, TILE_G, TILE_X0, TILE_GATE0 = 0, 1, 3, 6, 9
INPROJ_ROWS = 128
LANES = 128
CONV_STRIDE = 4


def _inproj_tile(j):
    return jnp.where(j < 2, 1 - j, j)


def _inproj_kernel(x_ref, mod_ref, w_ref, wdt_ref, lb_ref, cw_ref, cb_ref,
                   o_ref, lg_ref, dt_ref, u_ref, cpad_ref, ctmp_ref, *, tiles_per_batch):
    j = pl.program_id(1)
    tm = x_ref.shape[0]
    pieces = [slice(r, r + INPROJ_ROWS) for r in range(0, tm, INPROJ_ROWS)]

    @pl.when(j == 0)
    def _():
        m = mod_ref[0]
        lbp = lb_ref[...]
        lbe = jnp.exp(lbp - jnp.max(lbp, axis=0, keepdims=True))
        lb = lbe[0:1, :] / jnp.sum(lbe, axis=0, keepdims=True)
        for rs in pieces:
            u = _layer_norm(x_ref[rs, :]) * (1.0 + m[1:2, :]) + m[0:1, :]
            ub = u.astype(BF16)
            u_ref[rs, :] = ub
            dt_ref[rs, :] = _dot(ub, wdt_ref[...])
            f = lb + (1.0 - lb) * jax.nn.sigmoid(_dot(ub, w_ref[...]))
            lg_ref[rs, :] = jnp.log(f)
            o_ref[rs, :] = (1.0 - f).astype(o_ref.dtype)

    t = _inproj_tile(j)
    is_q = t == TILE_Q
    is_silu = is_q | ((t >= TILE_G) & (t < TILE_X0))
    is_gate = t >= TILE_GATE0

    @pl.when(is_silu)
    def _():
        scale = jnp.where(is_q, A_EXPAND ** -0.5, 1.0)
        for rs in pieces:
            acc = _dot(u_ref[rs, :], w_ref[...])
            o_ref[rs, :] = (acc * (jax.nn.sigmoid(acc) * scale)).astype(o_ref.dtype)

    @pl.when(is_gate)
    def _():
        for rs in pieces:
            o_ref[rs, :] = jax.nn.sigmoid(_dot(u_ref[rs, :], w_ref[...])).astype(o_ref.dtype)

    is_conv = (t >= TILE_X0) & (t < TILE_GATE0)
    seq_start = pl.program_id(0) % tiles_per_batch == 0
    n_slabs = w_ref.shape[1] // LANES

    def conv_tile(slab0):
        @pl.when(seq_start)
        def _():
            for s in range(n_slabs):
                cpad_ref[slab0 + s, 0:8, :] = jnp.zeros((8, LANES), F32)

        @pl.when(jnp.logical_not(seq_start))
        def _():
            for s in range(n_slabs):
                cpad_ref[slab0 + s, 0:8, :] = cpad_ref[slab0 + s, tm:tm + 8, :]

        n_str = INPROJ_ROWS // CONV_STRIDE
        for p, rs in enumerate(pieces):
            lo = 8 + rs.start
            acc = _dot(u_ref[rs, :], w_ref[...])
            for s in range(n_slabs):
                cpad_ref[slab0 + s, lo:lo + INPROJ_ROWS, :] = acc[:, s * LANES:(s + 1) * LANES]
            for s in range(n_slabs):
                ls = slice(s * LANES, (s + 1) * LANES)
                pad = cpad_ref.at[slab0 + s]
                out = ctmp_ref.at[p * n_slabs + s]
                for ph in range(CONV_STRIDE):
                    y = cb_ref[:, ls]
                    for k in range(B_CONV):
                        start = lo + ph - (B_CONV - 1) + k
                        y = y + cw_ref[k:k + 1, ls] * pad[pl.ds(start, n_str, stride=CONV_STRIDE), :]
                    out[pl.ds(ph, n_str, stride=CONV_STRIDE), :] = _silu(y)
            o_ref[rs, :] = jnp.concatenate(
                [ctmp_ref[p * n_slabs + s] for s in range(n_slabs)], axis=1).astype(o_ref.dtype)

    for kt in range(TILE_GATE0 - TILE_X0):
        pl.when(t == TILE_X0 + kt)(functools.partial(conv_tile, kt * n_slabs))

    @pl.when((j > 0) & jnp.logical_not(is_silu | is_gate | is_conv))
    def _():
        o_ref[...] = _dot(u_ref[...], w_ref[...]).astype(o_ref.dtype)


def _in_proj(x2, mod3, w_main, w_dt, hgrn_lb, conv_w, conv_b, seq):
    n_tok, d = x2.shape
    tm, tn = 1024, D_MODEL
    tiles_per_batch = seq // tm
    n_conv = TILE_GATE0 - TILE_X0
    conv_tile = lambda j: jnp.clip(_inproj_tile(j) - TILE_X0, 0, n_conv - 1)
    return pl.pallas_call(
        functools.partial(_inproj_kernel, tiles_per_batch=tiles_per_batch),
        grid=(n_tok // tm, MAIN_DIM // tn),
        in_specs=[pl.BlockSpec((tm, d), lambda i, j: (i, 0)),
                  pl.BlockSpec((1, 6, d), lambda i, j: (i // tiles_per_batch, 0, 0)),
                  pl.BlockSpec((d, tn), lambda i, j: (0, _inproj_tile(j))),
                  pl.BlockSpec((d, DT_PAD), lambda i, j: (0, 0)),
                  pl.BlockSpec(hgrn_lb.shape, lambda i, j: (0, 0)),
                  pl.BlockSpec((B_CONV, tn), lambda i, j: (0, conv_tile(j))),
                  pl.BlockSpec((1, tn), lambda i, j: (0, conv_tile(j)))],
        out_specs=[pl.BlockSpec((tm, tn), lambda i, j: (i, _inproj_tile(j))),
                   pl.BlockSpec((tm, tn), lambda i, j: (i, 0)),
                   pl.BlockSpec((tm, DT_PAD), lambda i, j: (i, 0))],
        out_shape=[jax.ShapeDtypeStruct((n_tok, MAIN_DIM), BF16),
                   jax.ShapeDtypeStruct((n_tok, tn), F32),
                   jax.ShapeDtypeStruct((n_tok, DT_PAD), F32)],
        scratch_shapes=[pltpu.VMEM((tm, d), BF16),
                        pltpu.VMEM((n_conv * (tn // LANES), tm + 8, LANES), F32),
                        pltpu.VMEM(((tm // INPROJ_ROWS) * (tn // LANES), INPROJ_ROWS, LANES),
                                   F32)],
        compiler_params=_cparams(("arbitrary", "arbitrary")),
        name="in_proj",
    )(x2, mod3, w_main, w_dt, hgrn_lb, conv_w, conv_b.reshape(1, -1))


def _hgrn_kernel(q_ref, k_ref, i_ref, g_ref, lg_ref, gn_ref, o_ref,
                 st_ref, qt_ref, qh_ref, kh_ref, kall_ref, vall_ref, dec_ref, oacc_ref,
                 *, n_chunks):
    @pl.when(pl.program_id(1) == 0)
    def _():
        st_ref[...] = jnp.zeros_like(st_ref)

    gn = gn_ref[...]
    tril = _tril_bf16(CHUNK)

    def prep(c, carry):
        r0 = c * CHUNK
        s0 = c * STACK
        rows = _ds(r0, CHUNK, CHUNK)
        kk = k_ref[0, rows, :].astype(F32)
        bc = _cumsum_rows(tril, lg_ref[0, rows, :])
        qf = q_ref[0, rows, :].astype(F32)
        vb = i_ref[0, rows, :]
        for blk in range(N_SUB):
            lo, hi = blk * SUB, (blk + 1) * SUB
            r = bc[lo - 1:lo, :] if blk > 0 else jnp.zeros((1, bc.shape[1]), F32)
            qt = qf[lo:hi] * jnp.exp(bc[lo:hi] - r)
            sub_rows = _ds(r0 + lo, SUB, SUB)
            qt_ref[sub_rows, :] = qt.astype(BF16)
            qh_ref[sub_rows, :] = (qt * jnp.exp(r)).astype(BF16)
            sec = _ds(s0 + SEC_OFF[blk], hi, SUB)
            kall_ref[sec, :] = (kk[:hi] * jnp.exp(jnp.minimum(r - bc[:hi], EXP_CLAMP))).astype(BF16)
            vall_ref[sec, :] = vb[:hi]
        b_last = bc[CHUNK - 1:CHUNK, :]
        kh_ref[rows, :] = (kk * jnp.exp(b_last - bc)).astype(BF16)
        dec_ref[_ds(c * 8, 8, 8), :] = jnp.broadcast_to(
            jnp.exp(b_last), (8, b_last.shape[1]))
        return carry

    for c in range(n_chunks):
        prep(c, 0)

    trow = lax.broadcasted_iota(jnp.int32, (CHUNK, STACK), 0)
    col = lax.broadcasted_iota(jnp.int32, (CHUNK, STACK), 1)
    valid = None
    for blk in range(N_SUB):
        in_sec = (col >= SEC_OFF[blk]) & (col < SEC_OFF[blk] + SUB * (blk + 1))
        ok = in_sec & (trow >= blk * SUB) & (trow < (blk + 1) * SUB) & (col - SEC_OFF[blk] <= trow)
        valid = ok if valid is None else (valid | ok)

    def scan(c, carry):
        rows = _ds(c * CHUNK, CHUNK, CHUNK)
        srows = _ds(c * STACK, STACK, SUB)
        vrows = _ds(c * STACK + SEC_OFF[-1], CHUNK, SUB)
        drow = pl.ds(c * 8, 1)
        heads = [slice(h * A_EXPAND, (h + 1) * A_EXPAND) for h in range(A_HEADS)]
        sc = [_dot_nt(qt_ref[rows, cs], kall_ref[srows, cs]) for cs in heads]
        st = [st_ref[h] for h in range(A_HEADS)]
        inter = [_dot_nt(qh_ref[rows, cs], st[h].astype(BF16)) for h, cs in enumerate(heads)]
        upd = [_dot_tn(vall_ref[vrows, cs], kh_ref[rows, cs]) for cs in heads]
        out = [_dot(jnp.where(valid, sc[h], 0.0).astype(BF16), vall_ref[srows, cs]) + inter[h]
               for h, cs in enumerate(heads)]
        new_st = [st[h] * dec_ref[drow, cs] + upd[h] for h, cs in enumerate(heads)]
        for h, cs in enumerate(heads):
            oacc_ref[rows, cs] = out[h]
            st_ref[h] = new_st[h]
        return carry

    for c in range(n_chunks):
        scan(c, 0)

    def fin(c, carry):
        rows = _ds(c * CHUNK, CHUNK, CHUNK)
        for h in range(A_HEADS):
            cols = slice(h * A_EXPAND, (h + 1) * A_EXPAND)
            o = oacc_ref[rows, cols]
            y = o * lax.rsqrt(jnp.mean(o * o, axis=-1, keepdims=True) + RMS_EPS) * gn
            o_ref[0, rows, cols] = (y * g_ref[0, rows, cols].astype(F32)).astype(o_ref.dtype)
        return carry

    for c in range(n_chunks):
        fin(c, 0)


def _hgrn(proj3, lg3, gnorm, t=256):
    bsz, seq, _ = proj3.shape
    nc = t // CHUNK
    blk = lambda off: pl.BlockSpec((1, t, D_MODEL), lambda b, s, o=off // D_MODEL: (b, s, o))
    return pl.pallas_call(
        functools.partial(_hgrn_kernel, n_chunks=nc),
        grid=(bsz, seq // t),
        in_specs=[blk(OFF_Q), blk(OFF_F), blk(OFF_I), blk(OFF_G), blk(0),
                  pl.BlockSpec((1, A_HEAD_I), lambda b, s: (0, 0))],
        out_specs=pl.BlockSpec((1, t, D_MODEL), lambda b, s: (b, s, 0)),
        out_shape=jax.ShapeDtypeStruct((bsz, seq, D_MODEL), BF16),
        scratch_shapes=[pltpu.VMEM((A_HEADS, A_HEAD_I, A_EXPAND), F32),
                        pltpu.VMEM((t, D_MODEL), BF16),
                        pltpu.VMEM((t, D_MODEL), BF16),
                        pltpu.VMEM((t, D_MODEL), BF16),
                        pltpu.VMEM((nc * STACK, D_MODEL), BF16),
                        pltpu.VMEM((nc * STACK, D_MODEL), BF16),
                        pltpu.VMEM((nc * 8, D_MODEL), F32),
                        pltpu.VMEM((t, D_MODEL), F32)],
        compiler_params=_cparams(("parallel", "arbitrary")),
        name="hgrn",
    )(proj3, proj3, proj3, proj3, lg3, gnorm.reshape(1, A_HEAD_I))


def _ssd_kernel(x_ref, bm_ref, cm_ref, z_ref, dt_ref, dtb_ref, alog_ref, dsk_ref, nw_ref,
                o_ref, xdt_ref, ax_ref, st_ref, *, n_chunks):
    grp = pl.program_id(1)

    @pl.when(pl.program_id(2) == 0)
    def _():
        st_ref[...] = jnp.zeros_like(st_ref)

    hrow = lax.broadcasted_iota(jnp.int32, (DT_PAD, B_GROUP_W), 0)
    hcol = lax.broadcasted_iota(jnp.int32, (DT_PAD, B_GROUP_W), 1) // B_HEADDIM + grp * B_HG
    expand = (hrow == hcol).astype(BF16)
    exp2 = jnp.concatenate([expand, expand], axis=0)

    def expand_row(ref):
        return _dot(_split_bf16(jnp.broadcast_to(ref[...], (8, DT_PAD))), exp2)[0:1, :]

    dtv = dt_ref[0] + dtb_ref[...]
    delta = jnp.maximum(dtv, 0.0) + jnp.log1p(jnp.exp(-jnp.abs(dtv)))
    delta_x = _dot(_split_bf16(delta), exp2)
    ax_ref[...] = delta_x * (-jnp.exp(expand_row(alog_ref)))
    xdt_ref[...] = x_ref[0].astype(F32) * delta_x

    tril = _tril_bf16(CHUNK)
    lane = lax.broadcasted_iota(jnp.int32, (CHUNK, 2 * B_HEADDIM), 1)
    trow = lax.broadcasted_iota(jnp.int32, (CHUNK, 2 * B_HEADDIM), 0)
    left = lane < B_HEADDIM
    spos = jnp.where(left, lane, lane - B_HEADDIM)
    causal2 = trow >= spos
    n_pairs = B_GROUP_W // (2 * B_HEADDIM)
    keep = jnp.tile((trow <= spos).astype(BF16), (1, 2 * n_pairs))
    ones8 = jnp.ones((8, CHUNK), BF16)

    dsk = expand_row(dsk_ref)

    def one_chunk(c, st):
        rows = _ds(c * CHUNK, CHUNK, CHUNK)
        bm = bm_ref[0, rows, :]
        cm = cm_ref[0, rows, :]
        xdt = xdt_ref[rows, :]
        axs = _split_bf16(ax_ref[rows, :])
        acx2 = _dot(tril, axs)
        acx = acx2[:, :B_GROUP_W] + acx2[:, B_GROUP_W:]
        a_last = acx[CHUNK - 1:CHUNK, :]
        kr2 = _dot(ones8, axs * keep)
        krow = kr2[0:1, :B_GROUP_W] + kr2[0:1, B_GROUP_W:]
        seg = acx - krow

        cb2 = _dot_nt(cm, jnp.concatenate([bm, bm], axis=0))
        inter = _dot(cm, st.astype(BF16)) * jnp.exp(acx)
        xw = (xdt * jnp.exp(a_last - acx)).astype(BF16)
        new_st = st * jnp.exp(a_last) + _dot_tn(bm, xw)
        y_tiles = []
        for p in range(n_pairs):
            cs = slice(p * 2 * B_HEADDIM, (p + 1) * 2 * B_HEADDIM)
            decay = jnp.exp(jnp.where(causal2, seg[:, cs], -jnp.inf))
            m = (cb2 * decay).astype(BF16)
            xp = xdt[:, cs]
            bd = jnp.concatenate([jnp.where(left, xp, 0.0), jnp.where(left, 0.0, xp)], axis=0)
            y_tiles.append(_dot(m, bd.astype(BF16)) + inter[:, cs])
        y = jnp.concatenate(y_tiles, axis=1)
        y = (y + x_ref[0, rows, :].astype(F32) * dsk) * z_ref[0, rows, :].astype(F32)
        y = y * lax.rsqrt(jnp.mean(y * y, axis=-1, keepdims=True) + RMS_EPS) * nw_ref[...]
        return y.astype(o_ref.dtype), new_st

    def body(i, carry):
        st = st_ref[...]
        outs = []
        for k in range(SSD_CHUNKS_PER_STEP):
            y, st = one_chunk(i * SSD_CHUNKS_PER_STEP + k, st)
            outs.append(y)
        for k, y in enumerate(outs):
            r0 = (i * SSD_CHUNKS_PER_STEP + k) * CHUNK
            o_ref[0, pl.ds(pl.multiple_of(r0, CHUNK), CHUNK), :] = y
        st_ref[...] = st
        return carry

    lax.fori_loop(0, n_chunks // SSD_CHUNKS_PER_STEP, body, 0)


def _ssd(proj3, dt3, dt_bias, a_log, d_skip, w_norm, t=512):
    bsz, seq, _ = proj3.shape
    nc = t // CHUNK
    gw, ns = B_GROUP_W, B_STATE
    pad32 = lambda v: jnp.pad(v.reshape(1, B_HEADS), ((0, 0), (0, DT_PAD - B_HEADS)))
    act = lambda width, off: pl.BlockSpec(
        (1, t, width), lambda b, g, s, o=off // width: (b, s, o + g))
    par = lambda rows, width, off: pl.BlockSpec(
        (rows, width), lambda b, g, s, o=off // width: (0, o + g))
    full = lambda shape: pl.BlockSpec(shape, lambda b, g, s: (0, 0))
    return pl.pallas_call(
        functools.partial(_ssd_kernel, n_chunks=nc),
        grid=(bsz, B_GROUPS, seq // t),
        in_specs=[act(gw, OFF_X), act(ns, OFF_B), act(ns, OFF_C), act(gw, OFF_Z),
                  pl.BlockSpec((1, t, DT_PAD), lambda b, g, s: (b, s, 0)),
                  full((1, DT_PAD)), full((1, DT_PAD)), full((1, DT_PAD)),
                  par(1, gw, 0)],
        out_specs=pl.BlockSpec((1, t, gw), lambda b, g, s: (b, s, g)),
        out_shape=jax.ShapeDtypeStruct((bsz, seq, B_INNER), BF16),
        scratch_shapes=[pltpu.VMEM((t, gw), F32),
                        pltpu.VMEM((t, gw), F32),
                        pltpu.VMEM((ns, gw), F32)],
        compiler_params=_cparams(("parallel", "parallel", "arbitrary")),
        name="ssd",
    )(proj3, proj3, proj3, proj3, dt3,
      pad32(dt_bias), pad32(a_log), pad32(d_skip), w_norm.reshape(1, B_INNER))


def _merge_kernel(ya_ref, yb_ref, ga_ref, gb_ref, x_ref, mod_ref, wa_ref, wb_ref, wo_ref,
                  lg_ref, lbias_ref, o_ref):
    m = mod_ref[0]
    for r in range(0, x_ref.shape[0], MERGE_ROWS):
        rs = slice(r, r + MERGE_ROWS)
        pa = _dot(ya_ref[rs, :], wa_ref[...])
        pb = _dot(yb_ref[rs, :], wb_ref[...])
        merged = ga_ref[rs, :].astype(F32) * pa + gb_ref[rs, :].astype(F32) * pb
        h = _dot(merged.astype(BF16), wo_ref[...])
        y = _layer_norm(ALPHA * x_ref[rs, :] + m[2:3, :] * h)
        o_ref[rs, :] = y * lg_ref[...] + lbias_ref[...]


MERGE_ROWS = 256


def _merge(ya2, yb2, proj2, x2, mod3, w_a, w_b, w_o, ln_g, ln_b, seq):
    n_tok, d = x2.shape
    tm = 1024
    tiles_per_batch = seq // tm
    row = lambda width, off=0: pl.BlockSpec((tm, width), lambda i, o=off // width: (i, o))
    full = lambda shape: pl.BlockSpec(shape, lambda i: (0, 0), pipeline_mode=pl.Buffered(1))
    return pl.pallas_call(
        _merge_kernel,
        grid=(n_tok // tm,),
        in_specs=[row(d), row(B_INNER), row(d, OFF_GA), row(d, OFF_GB), row(d),
                  pl.BlockSpec((1, 6, d), lambda i: (i // tiles_per_batch, 0, 0)),
                  full((d, d)), full((B_INNER, d)), full((d, d)),
                  full((1, d)), full((1, d))],
        out_specs=row(d),
        out_shape=jax.ShapeDtypeStruct((n_tok, d), F32),
        compiler_params=_cparams(("parallel",)),
        name="merge",
    )(ya2, yb2, proj2, proj2, x2, mod3, w_a, w_b, w_o, ln_g.reshape(1, d), ln_b.reshape(1, d))


FFN_ROWS = 256


def _ffn_kernel(x_ref, mod_ref, wg_ref, wu_ref, wd_ref, lg_ref, lbias_ref, o_ref):
    m = mod_ref[0]
    for r in range(0, x_ref.shape[0], FFN_ROWS):
        rs = slice(r, r + FFN_ROWS)
        x = x_ref[rs, :]
        u = (_layer_norm(x) * (1.0 + m[4:5, :]) + m[3:4, :]).astype(BF16)
        hidden = _silu(_dot(u, wg_ref[...])) * _dot(u, wu_ref[...])
        h = _dot(hidden.astype(BF16), wd_ref[...])
        y = _layer_norm(ALPHA * x + m[5:6, :] * h)
        o_ref[rs, :] = y * lg_ref[...] + lbias_ref[...]


def _ffn(x2, mod3, w_gate, w_up, w_down, ln_g, ln_b, seq):
    n_tok, d = x2.shape
    tm = 1024
    tiles_per_batch = seq // tm
    once = lambda shape: pl.BlockSpec(shape, lambda i: (0, 0), pipeline_mode=pl.Buffered(1))
    return pl.pallas_call(
        _ffn_kernel,
        grid=(n_tok // tm,),
        in_specs=[pl.BlockSpec((tm, d), lambda i: (i, 0)),
                  pl.BlockSpec((1, 6, d), lambda i: (i // tiles_per_batch, 0, 0)),
                  once((d, D_FF)), once((d, D_FF)), once((D_FF, d)),
                  once((1, d)), once((1, d))],
        out_specs=pl.BlockSpec((tm, d), lambda i: (i, 0)),
        out_shape=jax.ShapeDtypeStruct((n_tok, d), F32),
        compiler_params=_cparams(("parallel",)),
        name="ffn",
    )(x2, mod3, w_gate, w_up, w_down, ln_g.reshape(1, d), ln_b.reshape(1, d))


def kernel(x, c, w_ada, b_ada, w_in, hgrn_lb, hgrn_gnorm, ssm_conv_w, ssm_conv_b, ssm_dt_bias,
           ssm_a_log, ssm_d, ssm_norm, w_branch_a, w_branch_b, w_o, ln1_g, ln1_b,
           w_ffn_gate, w_ffn_up, w_ffn_down, ln2_g, ln2_b):
    bsz, seq, d = x.shape
    assert d == D_MODEL and w_ada.shape[0] == DEPTH == 1
    x2 = x.reshape(bsz * seq, d)
    for l in range(DEPTH):
        mod3 = _ada(c, w_ada[l], b_ada[l]).reshape(bsz, 6, d)
        w = w_in[l]
        dt_lo = OFF_X + B_INNER + 2 * B_GROUPS * B_STATE
        w_main = jnp.concatenate([w[:, :dt_lo], w[:, dt_lo + B_HEADS:]], axis=1).astype(BF16)
        w_dt = jnp.pad(w[:, dt_lo:dt_lo + B_HEADS], ((0, 0), (0, DT_PAD - B_HEADS))).astype(BF16)
        proj2, lg2, dt2 = _in_proj(x2, mod3, w_main, w_dt, hgrn_lb, ssm_conv_w[l], ssm_conv_b[l], seq)
        proj3 = proj2.reshape(bsz, seq, MAIN_DIM)
        ya = _hgrn(proj3, lg2.reshape(bsz, seq, D_MODEL), hgrn_gnorm[l])
        yb = _ssd(proj3, dt2.reshape(bsz, seq, DT_PAD),
                  ssm_dt_bias[l], ssm_a_log[l], ssm_d[l], ssm_norm[l])
        x2 = _merge(ya.reshape(bsz * seq, d), yb.reshape(bsz * seq, B_INNER), proj2, x2, mod3,
                    w_branch_a[l].astype(BF16), w_branch_b[l].astype(BF16), w_o[l].astype(BF16),
                    ln1_g[l], ln1_b[l], seq)
        x2 = _ffn(x2, mod3, w_ffn_gate[l].astype(BF16), w_ffn_up[l].astype(BF16),
                  w_ffn_down[l].astype(BF16), ln2_g[l], ln2_b[l], seq)
    return x2.reshape(bsz, seq, d)
```

```python
import functools

import jax
import jax.numpy as jnp
from jax import lax
from jax.experimental import pallas as pl
from jax.experimental.pallas import tpu as pltpu

F32 = jnp.float32
BF16 = jnp.bfloat16

D_MODEL = 1024
DEPTH = 1
A_HEADS = 8
A_EXPAND = 128
A_HEAD_I = 128
B_INNER = 2048
B_HEADDIM = 64
B_HEADS = 32
B_GROUPS = 4
B_HG = 8
B_STATE = 128
B_CONV = 4
B_GROUP_W = B_INNER // B_GROUPS
D_FF = 2816
ALPHA = (2.0 * DEPTH) ** 0.25
LN_EPS = 1e-5
RMS_EPS = 1e-6

OFF_Q, OFF_F, OFF_I, OFF_G = 0, 1024, 2048, 3072
OFF_Z = 4096
OFF_X = 6144
OFF_B = OFF_X + B_INNER
OFF_C = OFF_B + B_GROUPS * B_STATE
OFF_GA = 9216
OFF_GB = 10240
MAIN_DIM = 11264
DT_PAD = 128

CHUNK = 64
SUB = 16
N_SUB = CHUNK // SUB
SEC_OFF = tuple(SUB * i * (i + 1) // 2 for i in range(N_SUB))
STACK = SUB * N_SUB * (N_SUB + 1) // 2
EXP_CLAMP = 80.0
SSD_CHUNKS_PER_STEP = 8

VMEM_LIMIT = 56 * 1024 * 1024


def _cparams(sem):
    return pltpu.CompilerParams(dimension_semantics=sem, vmem_limit_bytes=VMEM_LIMIT)


def _layer_norm(x):
    mu = jnp.mean(x, axis=-1, keepdims=True)
    xc = x - mu
    return xc * lax.rsqrt(jnp.mean(xc * xc, axis=-1, keepdims=True) + LN_EPS)


def _silu(x):
    return x * jax.nn.sigmoid(x)


def _split_bf16(x, axis=-1):
    hi = lax.bitcast_convert_type(
        lax.bitcast_convert_type(x, jnp.uint32) & jnp.uint32(0xFFFF0000), F32)
    return jnp.concatenate([hi.astype(BF16), (x - hi).astype(BF16)], axis=axis)


def _dot(a, b):
    return jnp.dot(a, b, preferred_element_type=F32)


def _dot_nt(a, b):
    return lax.dot_general(a, b, (((1,), (1,)), ((), ())), preferred_element_type=F32)


def _dot_tn(a, b):
    return lax.dot_general(a, b, (((0,), (0,)), ((), ())), preferred_element_type=F32)


def _ds(start, size, align):
    if isinstance(start, int):
        return pl.ds(start, size)
    return pl.ds(pl.multiple_of(start, align), size)


def _tril_bf16(n):
    return (lax.broadcasted_iota(jnp.int32, (n, n), 0)
            >= lax.broadcasted_iota(jnp.int32, (n, n), 1)).astype(BF16)


def _cumsum_rows(tril, x):
    w = x.shape[1]
    y = _dot(tril, _split_bf16(x))
    return y[:, :w] + y[:, w:]


def _ada_kernel(c_ref, w_ref, b_ref, o_ref):
    cond = _silu(c_ref[...])
    o_ref[...] = jnp.dot(cond, w_ref[...], preferred_element_type=F32,
                         precision=lax.Precision.HIGHEST) + b_ref[...]


def _ada(c, w_ada, b_ada):
    bsz, d = c.shape
    n = w_ada.shape[1]
    tn = 1536
    return pl.pallas_call(
        _ada_kernel,
        grid=(n // tn,),
        in_specs=[pl.BlockSpec((bsz, d), lambda j: (0, 0)),
                  pl.BlockSpec((d, tn), lambda j: (0, j)),
                  pl.BlockSpec((1, tn), lambda j: (0, j))],
        out_specs=pl.BlockSpec((bsz, tn), lambda j: (0, j)),
        out_shape=jax.ShapeDtypeStruct((bsz, n), F32),
        compiler_params=_cparams(("arbitrary",)),
        name="ada",
    )(c, w_ada, b_ada.reshape(1, n))


TILE_Q, TILE_F, TILE_G, TILE_X0, TILE_GATE0 = 0, 1, 3, 6, 9
INPROJ_ROWS = 128
LANES = 128
CONV_STRIDE = 4


def _inproj_tile(j):
    return jnp.where(j < 2, 1 - j, j)


def _inproj_kernel(x_ref, mod_ref, w_ref, wg_ref, wdt_ref, lb_ref, cw_ref, cb_ref,
                   o_ref, lg_ref, dt_ref, u_ref, cpad_ref, ctmp_ref, *, tiles_per_batch):
    j = pl.program_id(1)
    tm = x_ref.shape[0]
    pieces = [slice(r, r + INPROJ_ROWS) for r in range(0, tm, INPROJ_ROWS)]

    @pl.when(j == 0)
    def _():
        m = mod_ref[0]
        lbp = lb_ref[...]
        lbe = jnp.exp(lbp - jnp.max(lbp, axis=0, keepdims=True))
        lb = lbe[0:1, :] / jnp.sum(lbe, axis=0, keepdims=True)
        wb = w_ref[...].astype(BF16)
        for rs in pieces:
            u = _layer_norm(x_ref[rs, :]) * (1.0 + m[1:2, :]) + m[0:1, :]
            ub = u.astype(BF16)
            u_ref[rs, :] = ub
            dt_ref[rs, :] = _dot(ub, wdt_ref[...])
            f = lb + (1.0 - lb) * jax.nn.sigmoid(_dot(ub, wb))
            lg_ref[rs, :] = jnp.log(f)
            o_ref[rs, :] = (1.0 - f).astype(o_ref.dtype)

    t = _inproj_tile(j)
    is_q = t == TILE_Q
    is_silu = is_q | ((t >= TILE_G) & (t < TILE_X0))
    is_gate = t >= TILE_GATE0

    @pl.when(is_silu)
    def _():
        scale = jnp.where(is_q, A_EXPAND ** -0.5, 1.0)
        wb = w_ref[...].astype(BF16)
        for rs in pieces:
            acc = _dot(u_ref[rs, :], wb)
            o_ref[rs, :] = (acc * (jax.nn.sigmoid(acc) * scale)).astype(o_ref.dtype)

    @pl.when(is_gate)
    def _():
        for rs in pieces:
            o_ref[rs, :] = jax.nn.sigmoid(_dot(u_ref[rs, :], wg_ref[...])).astype(o_ref.dtype)

    is_conv = (t >= TILE_X0) & (t < TILE_GATE0)
    seq_start = pl.program_id(0) % tiles_per_batch == 0
    n_slabs = w_ref.shape[1] // LANES

    def conv_tile(slab0):
        @pl.when(seq_start)
        def _():
            for s in range(n_slabs):
                cpad_ref[slab0 + s, 0:8, :] = jnp.zeros((8, LANES), F32)

        @pl.when(jnp.logical_not(seq_start))
        def _():
            for s in range(n_slabs):
                cpad_ref[slab0 + s, 0:8, :] = cpad_ref[slab0 + s, tm:tm + 8, :]

        n_str = INPROJ_ROWS // CONV_STRIDE
        wb = w_ref[...].astype(BF16)
        for p, rs in enumerate(pieces):
            lo = 8 + rs.start
            acc = _dot(u_ref[rs, :], wb)
            for s in range(n_slabs):
                cpad_ref[slab0 + s, lo:lo + INPROJ_ROWS, :] = acc[:, s * LANES:(s + 1) * LANES]
            for s in range(n_slabs):
                ls = slice(s * LANES, (s + 1) * LANES)
                pad = cpad_ref.at[slab0 + s]
                out = ctmp_ref.at[p * n_slabs + s]
                for ph in range(CONV_STRIDE):
                    y = cb_ref[:, ls]
                    for k in range(B_CONV):
                        start = lo + ph - (B_CONV - 1) + k
                        y = y + cw_ref[k:k + 1, ls] * pad[pl.ds(start, n_str, stride=CONV_STRIDE), :]
                    out[pl.ds(ph, n_str, stride=CONV_STRIDE), :] = _silu(y)
            o_ref[rs, :] = jnp.concatenate(
                [ctmp_ref[p * n_slabs + s] for s in range(n_slabs)], axis=1).astype(o_ref.dtype)

    for kt in range(TILE_GATE0 - TILE_X0):
        pl.when(t == TILE_X0 + kt)(functools.partial(conv_tile, kt * n_slabs))

    @pl.when((j > 0) & jnp.logical_not(is_silu | is_gate | is_conv))
    def _():
        o_ref[...] = _dot(u_ref[...], w_ref[...].astype(BF16)).astype(o_ref.dtype)


def _in_proj(x2, mod3, w_in, w_gates, w_dt, hgrn_lb, conv_w, conv_b, seq):
    n_tok, d = x2.shape
    tm, tn = 1024, D_MODEL
    tiles_per_batch = seq // tm
    n_conv = TILE_GATE0 - TILE_X0
    conv_tile = lambda j: jnp.clip(_inproj_tile(j) - TILE_X0, 0, n_conv - 1)
    return pl.pallas_call(
        functools.partial(_inproj_kernel, tiles_per_batch=tiles_per_batch),
        grid=(n_tok // tm, MAIN_DIM // tn),
        in_specs=[pl.BlockSpec((tm, d), lambda i, j: (i, 0)),
                  pl.BlockSpec((1, 6, d), lambda i, j: (i // tiles_per_batch, 0, 0)),
                  pl.BlockSpec((d, tn), lambda i, j: (0, jnp.minimum(_inproj_tile(j), TILE_GATE0 - 1))),
                  pl.BlockSpec((d, tn), lambda i, j: (0, jnp.maximum(_inproj_tile(j) - TILE_GATE0, 0))),
                  pl.BlockSpec((d, DT_PAD), lambda i, j: (0, 0)),
                  pl.BlockSpec(hgrn_lb.shape, lambda i, j: (0, 0)),
                  pl.BlockSpec((B_CONV, tn), lambda i, j: (0, conv_tile(j))),
                  pl.BlockSpec((1, tn), lambda i, j: (0, conv_tile(j)))],
        out_specs=[pl.BlockSpec((tm, tn), lambda i, j: (i, _inproj_tile(j))),
                   pl.BlockSpec((tm, tn), lambda i, j: (i, 0)),
                   pl.BlockSpec((tm, DT_PAD), lambda i, j: (i, 0))],
        out_shape=[jax.ShapeDtypeStruct((n_tok, MAIN_DIM), BF16),
                   jax.ShapeDtypeStruct((n_tok, tn), F32),
                   jax.ShapeDtypeStruct((n_tok, DT_PAD), F32)],
        scratch_shapes=[pltpu.VMEM((tm, d), BF16),
                        pltpu.VMEM((n_conv * (tn // LANES), tm + 8, LANES), F32),
                        pltpu.VMEM(((tm // INPROJ_ROWS) * (tn // LANES), INPROJ_ROWS, LANES),
                                   F32)],
        compiler_params=_cparams(("arbitrary", "arbitrary")),
        name="in_proj",
    )(x2, mod3, w_in, w_gates, w_dt, hgrn_lb, conv_w, conv_b.reshape(1, -1))


def _hgrn_kernel(q_ref, k_ref, i_ref, g_ref, lg_ref, gn_ref, o_ref,
                 st_ref, qt_ref, qh_ref, kh_ref, kall_ref, vall_ref, dec_ref, oacc_ref,
                 *, n_chunks):
    @pl.when(pl.program_id(1) == 0)
    def _():
        st_ref[...] = jnp.zeros_like(st_ref)

    gn = gn_ref[...]
    tril = _tril_bf16(CHUNK)

    def prep(c, carry):
        r0 = c * CHUNK
        s0 = c * STACK
        rows = _ds(r0, CHUNK, CHUNK)
        kk = k_ref[0, rows, :].astype(F32)
        bc = _cumsum_rows(tril, lg_ref[0, rows, :])
        qf = q_ref[0, rows, :].astype(F32)
        vb = i_ref[0, rows, :]
        for blk in range(N_SUB):
            lo, hi = blk * SUB, (blk + 1) * SUB
            r = bc[lo - 1:lo, :] if blk > 0 else jnp.zeros((1, bc.shape[1]), F32)
            qt = qf[lo:hi] * jnp.exp(bc[lo:hi] - r)
            sub_rows = _ds(r0 + lo, SUB, SUB)
            qt_ref[sub_rows, :] = qt.astype(BF16)
            qh_ref[sub_rows, :] = (qt * jnp.exp(r)).astype(BF16)
            sec = _ds(s0 + SEC_OFF[blk], hi, SUB)
            kall_ref[sec, :] = (kk[:hi] * jnp.exp(jnp.minimum(r - bc[:hi], EXP_CLAMP))).astype(BF16)
            vall_ref[sec, :] = vb[:hi]
        b_last = bc[CHUNK - 1:CHUNK, :]
        kh_ref[rows, :] = (kk * jnp.exp(b_last - bc)).astype(BF16)
        dec_ref[_ds(c * 8, 8, 8), :] = jnp.broadcast_to(
            jnp.exp(b_last), (8, b_last.shape[1]))
        return carry

    for c in range(n_chunks):
        prep(c, 0)

    trow = lax.broadcasted_iota(jnp.int32, (CHUNK, STACK), 0)
    col = lax.broadcasted_iota(jnp.int32, (CHUNK, STACK), 1)
    valid = None
    for blk in range(N_SUB):
        in_sec = (col >= SEC_OFF[blk]) & (col < SEC_OFF[blk] + SUB * (blk + 1))
        ok = in_sec & (trow >= blk * SUB) & (trow < (blk + 1) * SUB) & (col - SEC_OFF[blk] <= trow)
        valid = ok if valid is None else (valid | ok)

    def scan(c, carry):
        rows = _ds(c * CHUNK, CHUNK, CHUNK)
        srows = _ds(c * STACK, STACK, SUB)
        vrows = _ds(c * STACK + SEC_OFF[-1], CHUNK, SUB)
        drow = pl.ds(c * 8, 1)
        heads = [slice(h * A_EXPAND, (h + 1) * A_EXPAND) for h in range(A_HEADS)]
        sc = [_dot_nt(qt_ref[rows, cs], kall_ref[srows, cs]) for cs in heads]
        st = [st_ref[h] for h in range(A_HEADS)]
        inter = [_dot_nt(qh_ref[rows, cs], st[h].astype(BF16)) for h, cs in enumerate(heads)]
        upd = [_dot_tn(vall_ref[vrows, cs], kh_ref[rows, cs]) for cs in heads]
        out = [_dot(jnp.where(valid, sc[h], 0.0).astype(BF16), vall_ref[srows, cs]) + inter[h]
               for h, cs in enumerate(heads)]
        new_st = [st[h] * dec_ref[drow, cs] + upd[h] for h, cs in enumerate(heads)]
        for h, cs in enumerate(heads):
            oacc_ref[rows, cs] = out[h]
            st_ref[h] = new_st[h]
        return carry

    for c in range(n_chunks):
        scan(c, 0)

    def fin(c, carry):
        rows = _ds(c * CHUNK, CHUNK, CHUNK)
        for h in range(A_HEADS):
            cols = slice(h * A_EXPAND, (h + 1) * A_EXPAND)
            o = oacc_ref[rows, cols]
            y = o * lax.rsqrt(jnp.mean(o * o, axis=-1, keepdims=True) + RMS_EPS) * gn
            o_ref[0, rows, cols] = (y * g_ref[0, rows, cols].astype(F32)).astype(o_ref.dtype)
        return carry

    for c in range(n_chunks):
        fin(c, 0)


def _hgrn(proj3, lg3, gnorm, t=256):
    bsz, seq, _ = proj3.shape
    nc = t // CHUNK
    blk = lambda off: pl.BlockSpec((1, t, D_MODEL), lambda b, s, o=off // D_MODEL: (b, s, o))
    return pl.pallas_call(
        functools.partial(_hgrn_kernel, n_chunks=nc),
        grid=(bsz, seq // t),
        in_specs=[blk(OFF_Q), blk(OFF_F), blk(OFF_I), blk(OFF_G), blk(0),
                  pl.BlockSpec((1, A_HEAD_I), lambda b, s: (0, 0))],
        out_specs=pl.BlockSpec((1, t, D_MODEL), lambda b, s: (b, s, 0)),
        out_shape=jax.ShapeDtypeStruct((bsz, seq, D_MODEL), BF16),
        scratch_shapes=[pltpu.VMEM((A_HEADS, A_HEAD_I, A_EXPAND), F32),
                        pltpu.VMEM((t, D_MODEL), BF16),
                        pltpu.VMEM((t, D_MODEL), BF16),
                        pltpu.VMEM((t, D_MODEL), BF16),
                        pltpu.VMEM((nc * STACK, D_MODEL), BF16),
                        pltpu.VMEM((nc * STACK, D_MODEL), BF16),
                        pltpu.VMEM((nc * 8, D_MODEL), F32),
                        pltpu.VMEM((t, D_MODEL), F32)],
        compiler_params=_cparams(("parallel", "arbitrary")),
        name="hgrn",
    )(proj3, proj3, proj3, proj3, lg3, gnorm.reshape(1, A_HEAD_I))


def _ssd_kernel(x_ref, bm_ref, cm_ref, z_ref, dt_ref, dtb_ref, alog_ref, dsk_ref, nw_ref,
                o_ref, xdt_ref, ax_ref, st_ref, *, n_chunks):
    grp = pl.program_id(1)

    @pl.when(pl.program_id(2) == 0)
    def _():
        st_ref[...] = jnp.zeros_like(st_ref)

    hrow = lax.broadcasted_iota(jnp.int32, (DT_PAD, B_GROUP_W), 0)
    hcol = lax.broadcasted_iota(jnp.int32, (DT_PAD, B_GROUP_W), 1) // B_HEADDIM + grp * B_HG
    expand = (hrow == hcol).astype(BF16)
    exp2 = jnp.concatenate([expand, expand], axis=0)

    def expand_row(ref):
        return _dot(_split_bf16(jnp.broadcast_to(ref[...], (8, DT_PAD))), exp2)[0:1, :]

    dtv = dt_ref[0] + dtb_ref[...]
    delta = jnp.maximum(dtv, 0.0) + jnp.log1p(jnp.exp(-jnp.abs(dtv)))
    delta_x = _dot(_split_bf16(delta), exp2)
    ax_ref[...] = delta_x * (-jnp.exp(expand_row(alog_ref)))
    xdt_ref[...] = x_ref[0].astype(F32) * delta_x

    tril = _tril_bf16(CHUNK)
    lane = lax.broadcasted_iota(jnp.int32, (CHUNK, 2 * B_HEADDIM), 1)
    trow = lax.broadcasted_iota(jnp.int32, (CHUNK, 2 * B_HEADDIM), 0)
    left = lane < B_HEADDIM
    spos = jnp.where(left, lane, lane - B_HEADDIM)
    causal2 = trow >= spos
    n_pairs = B_GROUP_W // (2 * B_HEADDIM)
    keep = jnp.tile((trow <= spos).astype(BF16), (1, 2 * n_pairs))
    ones8 = jnp.ones((8, CHUNK), BF16)

    dsk = expand_row(dsk_ref)

    def one_chunk(c, st):
        rows = _ds(c * CHUNK, CHUNK, CHUNK)
        bm = bm_ref[0, rows, :]
        cm = cm_ref[0, rows, :]
        xdt = xdt_ref[rows, :]
        axs = _split_bf16(ax_ref[rows, :])
        acx2 = _dot(tril, axs)
        acx = acx2[:, :B_GROUP_W] + acx2[:, B_GROUP_W:]
        a_last = acx[CHUNK - 1:CHUNK, :]
        kr2 = _dot(ones8, axs * keep)
        krow = kr2[0:1, :B_GROUP_W] + kr2[0:1, B_GROUP_W:]
        seg = acx - krow

        cb2 = _dot_nt(cm, jnp.concatenate([bm, bm], axis=0))
        inter = _dot(cm, st.astype(BF16)) * jnp.exp(acx)
        xw = (xdt * jnp.exp(a_last - acx)).astype(BF16)
        new_st = st * jnp.exp(a_last) + _dot_tn(bm, xw)
        y_tiles = []
        for p in range(n_pairs):
            cs = slice(p * 2 * B_HEADDIM, (p + 1) * 2 * B_HEADDIM)
            decay = jnp.exp(jnp.where(causal2, seg[:, cs], -jnp.inf))
            m = (cb2 * decay).astype(BF16)
            xp = xdt[:, cs]
            bd = jnp.concatenate([jnp.where(left, xp, 0.0), jnp.where(left, 0.0, xp)], axis=0)
            y_tiles.append(_dot(m, bd.astype(BF16)) + inter[:, cs])
        y = jnp.concatenate(y_tiles, axis=1)
        y = (y + x_ref[0, rows, :].astype(F32) * dsk) * z_ref[0, rows, :].astype(F32)
        y = y * lax.rsqrt(jnp.mean(y * y, axis=-1, keepdims=True) + RMS_EPS) * nw_ref[...]
        return y.astype(o_ref.dtype), new_st

    def body(i, carry):
        st = st_ref[...]
        outs = []
        for k in range(SSD_CHUNKS_PER_STEP):
            y, st = one_chunk(i * SSD_CHUNKS_PER_STEP + k, st)
            outs.append(y)
        for k, y in enumerate(outs):
            r0 = (i * SSD_CHUNKS_PER_STEP + k) * CHUNK
            o_ref[0, pl.ds(pl.multiple_of(r0, CHUNK), CHUNK), :] = y
        st_ref[...] = st
        return carry

    lax.fori_loop(0, n_chunks // SSD_CHUNKS_PER_STEP, body, 0)


def _ssd(proj3, dt3, dt_bias, a_log, d_skip, w_norm, t=512):
    bsz, seq, _ = proj3.shape
    nc = t // CHUNK
    gw, ns = B_GROUP_W, B_STATE
    pad32 = lambda v: jnp.pad(v.reshape(1, B_HEADS), ((0, 0), (0, DT_PAD - B_HEADS)))
    act = lambda width, off: pl.BlockSpec(
        (1, t, width), lambda b, g, s, o=off // width: (b, s, o + g))
    par = lambda rows, width, off: pl.BlockSpec(
        (rows, width), lambda b, g, s, o=off // width: (0, o + g))
    full = lambda shape: pl.BlockSpec(shape, lambda b, g, s: (0, 0))
    return pl.pallas_call(
        functools.partial(_ssd_kernel, n_chunks=nc),
        grid=(bsz, B_GROUPS, seq // t),
        in_specs=[act(gw, OFF_X), act(ns, OFF_B), act(ns, OFF_C), act(gw, OFF_Z),
                  pl.BlockSpec((1, t, DT_PAD), lambda b, g, s: (b, s, 0)),
                  full((1, DT_PAD)), full((1, DT_PAD)), full((1, DT_PAD)),
                  par(1, gw, 0)],
        out_specs=pl.BlockSpec((1, t, gw), lambda b, g, s: (b, s, g)),
        out_shape=jax.ShapeDtypeStruct((bsz, seq, B_INNER), BF16),
        scratch_shapes=[pltpu.VMEM((t, gw), F32),
                        pltpu.VMEM((t, gw), F32),
                        pltpu.VMEM((ns, gw), F32)],
        compiler_params=_cparams(("parallel", "parallel", "arbitrary")),
        name="ssd",
    )(proj3, proj3, proj3, proj3, dt3,
      pad32(dt_bias), pad32(a_log), pad32(d_skip), w_norm.reshape(1, B_INNER))


MERGE_ROWS = 256


def _merge_kernel(ya_ref, yb_ref, ga_ref, gb_ref, x_ref, mod_ref, wa_ref, wb_ref, wo_ref,
                  lg_ref, lbias_ref, o_ref):
    m = mod_ref[0]
    for r in range(0, x_ref.shape[0], MERGE_ROWS):
        rs = slice(r, r + MERGE_ROWS)
        pa = _dot(ya_ref[rs, :], wa_ref[...])
        pb = _dot(yb_ref[rs, :], wb_ref[...])
        merged = ga_ref[rs, :].astype(F32) * pa + gb_ref[rs, :].astype(F32) * pb
        h = _dot(merged.astype(BF16), wo_ref[...])
        y = _layer_norm(ALPHA * x_ref[rs, :] + m[2:3, :] * h)
        o_ref[rs, :] = y * lg_ref[...] + lbias_ref[...]


def _merge(ya2, yb2, proj2, x2, mod3, w_a, w_b, w_o, ln_g, ln_b, seq):
    n_tok, d = x2.shape
    tm = 1024
    tiles_per_batch = seq // tm
    row = lambda width, off=0: pl.BlockSpec((tm, width), lambda i, o=off // width: (i, o))
    full = lambda shape: pl.BlockSpec(shape, lambda i: (0, 0), pipeline_mode=pl.Buffered(1))
    return pl.pallas_call(
        _merge_kernel,
        grid=(n_tok // tm,),
        in_specs=[row(d), row(B_INNER), row(d, OFF_GA), row(d, OFF_GB), row(d),
                  pl.BlockSpec((1, 6, d), lambda i: (i // tiles_per_batch, 0, 0)),
                  full((d, d)), full((B_INNER, d)), full((d, d)),
                  full((1, d)), full((1, d))],
        out_specs=row(d),
        out_shape=jax.ShapeDtypeStruct((n_tok, d), F32),
        compiler_params=_cparams(("parallel",)),
        name="merge",
    )(ya2, yb2, proj2, proj2, x2, mod3, w_a, w_b, w_o, ln_g.reshape(1, d), ln_b.reshape(1, d))


FFN_ROWS = 256


def _ffn_kernel(x_ref, mod_ref, wg_ref, wu_ref, wd_ref, lg_ref, lbias_ref, o_ref):
    m = mod_ref[0]
    for r in range(0, x_ref.shape[0], FFN_ROWS):
        rs = slice(r, r + FFN_ROWS)
        x = x_ref[rs, :]
        u = (_layer_norm(x) * (1.0 + m[4:5, :]) + m[3:4, :]).astype(BF16)
        hidden = _silu(_dot(u, wg_ref[...])) * _dot(u, wu_ref[...])
        h = _dot(hidden.astype(BF16), wd_ref[...])
        y = _layer_norm(ALPHA * x + m[5:6, :] * h)
        o_ref[rs, :] = y * lg_ref[...] + lbias_ref[...]


def _ffn(x2, mod3, w_gate, w_up, w_down, ln_g, ln_b, seq):
    n_tok, d = x2.shape
    tm = 1024
    tiles_per_batch = seq // tm
    once = lambda shape: pl.BlockSpec(shape, lambda i: (0, 0), pipeline_mode=pl.Buffered(1))
    return pl.pallas_call(
        _ffn_kernel,
        grid=(n_tok // tm,),
        in_specs=[pl.BlockSpec((tm, d), lambda i: (i, 0)),
                  pl.BlockSpec((1, 6, d), lambda i: (i // tiles_per_batch, 0, 0)),
                  once((d, D_FF)), once((d, D_FF)), once((D_FF, d)),
                  once((1, d)), once((1, d))],
        out_specs=pl.BlockSpec((tm, d), lambda i: (i, 0)),
        out_shape=jax.ShapeDtypeStruct((n_tok, d), F32),
        compiler_params=_cparams(("parallel",)),
        name="ffn",
    )(x2, mod3, w_gate, w_up, w_down, ln_g.reshape(1, d), ln_b.reshape(1, d))


def kernel(x, c, w_ada, b_ada, w_in, hgrn_lb, hgrn_gnorm, ssm_conv_w, ssm_conv_b, ssm_dt_bias,
           ssm_a_log, ssm_d, ssm_norm, w_branch_a, w_branch_b, w_o, ln1_g, ln1_b,
           w_ffn_gate, w_ffn_up, w_ffn_down, ln2_g, ln2_b):
    bsz, seq, d = x.shape
    assert d == D_MODEL and w_ada.shape[0] == DEPTH == 1
    x2 = x.reshape(bsz * seq, d)
    for l in range(DEPTH):
        mod3 = _ada(c, w_ada[l], b_ada[l]).reshape(bsz, 6, d)
        w = w_in[l]
        dt_lo = OFF_X + B_INNER + 2 * B_GROUPS * B_STATE
        w_gates = w[:, dt_lo + B_HEADS:].astype(BF16)
        w_dt = jnp.pad(w[:, dt_lo:dt_lo + B_HEADS], ((0, 0), (0, DT_PAD - B_HEADS))).astype(BF16)
        proj2, lg2, dt2 = _in_proj(x2, mod3, w, w_gates, w_dt, hgrn_lb,
                                   ssm_conv_w[l], ssm_conv_b[l], seq)
        proj3 = proj2.reshape(bsz, seq, MAIN_DIM)
        ya = _hgrn(proj3, lg2.reshape(bsz, seq, D_MODEL), hgrn_gnorm[l])
        yb = _ssd(proj3, dt2.reshape(bsz, seq, DT_PAD),
                  ssm_dt_bias[l], ssm_a_log[l], ssm_d[l], ssm_norm[l])
        x2 = _merge(ya.reshape(bsz * seq, d), yb.reshape(bsz * seq, B_INNER), proj2, x2, mod3,
                    w_branch_a[l].astype(BF16), w_branch_b[l].astype(BF16), w_o[l].astype(BF16),
                    ln1_g[l], ln1_b[l], seq)
        x2 = _ffn(x2, mod3, w_ffn_gate[l].astype(BF16), w_ffn_up[l].astype(BF16),
                  w_ffn_down[l].astype(BF16), ln2_g[l], ln2_b[l], seq)
    return x2.reshape(bsz, seq, d)
```

```python
import functools

import jax
import jax.numpy as jnp
from jax import lax
from jax.experimental import pallas as pl
from jax.experimental.pallas import tpu as pltpu

F32 = jnp.float32
BF16 = jnp.bfloat16

D_MODEL = 1024
DEPTH = 1
A_HEADS = 8
A_EXPAND = 128
A_HEAD_I = 128
B_INNER = 2048
B_HEADDIM = 64
B_HEADS = 32
B_GROUPS = 4
B_HG = 8
B_STATE = 128
B_CONV = 4
B_GROUP_W = B_INNER // B_GROUPS
D_FF = 2816
ALPHA = (2.0 * DEPTH) ** 0.25
LN_EPS = 1e-5
RMS_EPS = 1e-6

OFF_Q, OFF_F, OFF_I, OFF_G = 0, 1024, 2048, 3072
OFF_Z = 4096
OFF_X = 6144
OFF_B = OFF_X + B_INNER
OFF_C = OFF_B + B_GROUPS * B_STATE
OFF_GA = 9216
OFF_GB = 10240
MAIN_DIM = 11264
DT_PAD = 128

CHUNK = 64
SUB = 16
N_SUB = CHUNK // SUB
SEC_OFF = tuple(SUB * i * (i + 1) // 2 for i in range(N_SUB))
STACK = SUB * N_SUB * (N_SUB + 1) // 2
EXP_CLAMP = 80.0
SSD_CHUNKS_PER_STEP = 8

VMEM_LIMIT = 56 * 1024 * 1024


def _cparams(sem):
    return pltpu.CompilerParams(dimension_semantics=sem, vmem_limit_bytes=VMEM_LIMIT)


def _layer_norm(x):
    mu = jnp.mean(x, axis=-1, keepdims=True)
    xc = x - mu
    return xc * lax.rsqrt(jnp.mean(xc * xc, axis=-1, keepdims=True) + LN_EPS)


def _silu(x):
    return x * jax.nn.sigmoid(x)


def _split_bf16(x, axis=-1):
    hi = lax.bitcast_convert_type(
        lax.bitcast_convert_type(x, jnp.uint32) & jnp.uint32(0xFFFF0000), F32)
    return jnp.concatenate([hi.astype(BF16), (x - hi).astype(BF16)], axis=axis)


def _dot(a, b):
    return jnp.dot(a, b, preferred_element_type=F32)


def _dot_nt(a, b):
    return lax.dot_general(a, b, (((1,), (1,)), ((), ())), preferred_element_type=F32)


def _dot_tn(a, b):
    return lax.dot_general(a, b, (((0,), (0,)), ((), ())), preferred_element_type=F32)


def _ds(start, size, align):
    if isinstance(start, int):
        return pl.ds(start, size)
    return pl.ds(pl.multiple_of(start, align), size)


def _tril_bf16(n):
    return (lax.broadcasted_iota(jnp.int32, (n, n), 0)
            >= lax.broadcasted_iota(jnp.int32, (n, n), 1)).astype(BF16)


def _cumsum_rows(tril, x):
    w = x.shape[1]
    y = _dot(tril, _split_bf16(x))
    return y[:, :w] + y[:, w:]


def _ada_kernel(c_ref, w_ref, b_ref, o_ref):
    cond = _silu(c_ref[...])
    o_ref[...] = jnp.dot(cond, w_ref[...], preferred_element_type=F32,
                         precision=lax.Precision.HIGHEST) + b_ref[...]


def _ada(c, w_ada, b_ada):
    bsz, d = c.shape
    n = w_ada.shape[1]
    tn = 1536
    return pl.pallas_call(
        _ada_kernel,
        grid=(n // tn,),
        in_specs=[pl.BlockSpec((bsz, d), lambda j: (0, 0)),
                  pl.BlockSpec((d, tn), lambda j: (0, j)),
                  pl.BlockSpec((1, tn), lambda j: (0, j))],
        out_specs=pl.BlockSpec((bsz, tn), lambda j: (0, j)),
        out_shape=jax.ShapeDtypeStruct((bsz, n), F32),
        compiler_params=_cparams(("arbitrary",)),
        name="ada",
    )(c, w_ada, b_ada.reshape(1, n))


TILE_Q, TILE_F, TILE_G, TILE_X0, TILE_GATE0 = 0, 1, 3, 6, 9
INPROJ_ROWS = 128
LANES = 128
CONV_STRIDE = 4


def _inproj_tile(j):
    return jnp.where(j < 2, 1 - j, j)


def _inproj_kernel(x_ref, mod_ref, w_ref, wdt_ref, lb_ref, cw_ref, cb_ref,
                   o_ref, lg_ref, dt_ref, u_ref, cpad_ref, ctmp_ref, *, tiles_per_batch):
    j = pl.program_id(1)
    tm = x_ref.shape[0]
    pieces = [slice(r, r + INPROJ_ROWS) for r in range(0, tm, INPROJ_ROWS)]

    @pl.when(j == 0)
    def _():
        m = mod_ref[0]
        lbp = lb_ref[...]
        lbe = jnp.exp(lbp - jnp.max(lbp, axis=0, keepdims=True))
        lb = lbe[0:1, :] / jnp.sum(lbe, axis=0, keepdims=True)
        for rs in pieces:
            u = _layer_norm(x_ref[rs, :]) * (1.0 + m[1:2, :]) + m[0:1, :]
            ub = u.astype(BF16)
            u_ref[rs, :] = ub
            dt_ref[rs, :] = _dot(ub, wdt_ref[...])
            f = lb + (1.0 - lb) * jax.nn.sigmoid(_dot(ub, w_ref[...]))
            lg_ref[rs, :] = jnp.log(f)
            o_ref[rs, :] = (1.0 - f).astype(o_ref.dtype)

    t = _inproj_tile(j)
    is_q = t == TILE_Q
    is_silu = is_q | ((t >= TILE_G) & (t < TILE_X0))
    is_gate = t >= TILE_GATE0

    @pl.when(is_silu)
    def _():
        scale = jnp.where(is_q, A_EXPAND ** -0.5, 1.0)
        for rs in pieces:
            acc = _dot(u_ref[rs, :], w_ref[...])
            o_ref[rs, :] = (acc * (jax.nn.sigmoid(acc) * scale)).astype(o_ref.dtype)

    @pl.when(is_gate)
    def _():
        for rs in pieces:
            o_ref[rs, :] = jax.nn.sigmoid(_dot(u_ref[rs, :], w_ref[...])).astype(o_ref.dtype)

    is_conv = (t >= TILE_X0) & (t < TILE_GATE0)
    seq_start = pl.program_id(0) % tiles_per_batch == 0
    n_slabs = w_ref.shape[1] // LANES

    def conv_tile(slab0):
        @pl.when(seq_start)
        def _():
            for s in range(n_slabs):
                cpad_ref[slab0 + s, 0:8, :] = jnp.zeros((8, LANES), F32)

        @pl.when(jnp.logical_not(seq_start))
        def _():
            for s in range(n_slabs):
                cpad_ref[slab0 + s, 0:8, :] = cpad_ref[slab0 + s, tm:tm + 8, :]

        n_str = INPROJ_ROWS // CONV_STRIDE
        for p, rs in enumerate(pieces):
            lo = 8 + rs.start
            acc = _dot(u_ref[rs, :], w_ref[...])
            for s in range(n_slabs):
                cpad_ref[slab0 + s, lo:lo + INPROJ_ROWS, :] = acc[:, s * LANES:(s + 1) * LANES]
            for s in range(n_slabs):
                ls = slice(s * LANES, (s + 1) * LANES)
                pad = cpad_ref.at[slab0 + s]
                out = ctmp_ref.at[p * n_slabs + s]
                for ph in range(CONV_STRIDE):
                    y = cb_ref[:, ls]
                    for k in range(B_CONV):
                        start = lo + ph - (B_CONV - 1) + k
                        y = y + cw_ref[k:k + 1, ls] * pad[pl.ds(start, n_str, stride=CONV_STRIDE), :]
                    out[pl.ds(ph, n_str, stride=CONV_STRIDE), :] = _silu(y)
            o_ref[rs, :] = jnp.concatenate(
                [ctmp_ref[p * n_slabs + s] for s in range(n_slabs)], axis=1).astype(o_ref.dtype)

    for kt in range(TILE_GATE0 - TILE_X0):
        pl.when(t == TILE_X0 + kt)(functools.partial(conv_tile, kt * n_slabs))

    @pl.when((j > 0) & jnp.logical_not(is_silu | is_gate | is_conv))
    def _():
        o_ref[...] = _dot(u_ref[...], w_ref[...]).astype(o_ref.dtype)


def _in_proj(x2, mod3, w_main, w_dt, hgrn_lb, conv_w, conv_b, seq):
    n_tok, d = x2.shape
    tm, tn = 1024, D_MODEL
    tiles_per_batch = seq // tm
    n_conv = TILE_GATE0 - TILE_X0
    conv_tile = lambda j: jnp.clip(_inproj_tile(j) - TILE_X0, 0, n_conv - 1)
    return pl.pallas_call(
        functools.partial(_inproj_kernel, tiles_per_batch=tiles_per_batch),
        grid=(n_tok // tm, MAIN_DIM // tn),
        in_specs=[pl.BlockSpec((tm, d), lambda i, j: (i, 0)),
                  pl.BlockSpec((1, 6, d), lambda i, j: (i // tiles_per_batch, 0, 0)),
                  pl.BlockSpec((d, tn), lambda i, j: (0, _inproj_tile(j))),
                  pl.BlockSpec((d, DT_PAD), lambda i, j: (0, 0)),
                  pl.BlockSpec(hgrn_lb.shape, lambda i, j: (0, 0)),
                  pl.BlockSpec((B_CONV, tn), lambda i, j: (0, conv_tile(j))),
                  pl.BlockSpec((1, tn), lambda i, j: (0, conv_tile(j)))],
        out_specs=[pl.BlockSpec((tm, tn), lambda i, j: (i, _inproj_tile(j))),
                   pl.BlockSpec((tm, tn), lambda i, j: (i, 0)),
                   pl.BlockSpec((tm, DT_PAD), lambda i, j: (i, 0))],
        out_shape=[jax.ShapeDtypeStruct((n_tok, MAIN_DIM), BF16),
                   jax.ShapeDtypeStruct((n_tok, tn), F32),
                   jax.ShapeDtypeStruct((n_tok, DT_PAD), F32)],
        scratch_shapes=[pltpu.VMEM((tm, d), BF16),
                        pltpu.VMEM((n_conv * (tn // LANES), tm + 8, LANES), F32),
                        pltpu.VMEM(((tm // INPROJ_ROWS) * (tn // LANES), INPROJ_ROWS, LANES),
                                   F32)],
        compiler_params=_cparams(("arbitrary", "arbitrary")),
        name="in_proj",
    )(x2, mod3, w_main, w_dt, hgrn_lb, conv_w, conv_b.reshape(1, -1))


def _hgrn_kernel(q_ref, k_ref, i_ref, g_ref, lg_ref, gn_ref, o_ref,
                 st_ref, qt_ref, qh_ref, kh_ref, kall_ref, vall_ref, dec_ref, oacc_ref, st0_ref,
                 *, n_chunks):
    @pl.when(pl.program_id(1) == 0)
    def _():
        st_ref[...] = jnp.zeros_like(st_ref)

    st0_ref[...] = st_ref[...]
    gn = gn_ref[...]
    tril = _tril_bf16(CHUNK)

    def prep(c, worst):
        r0 = c * CHUNK
        s0 = c * STACK
        rows = _ds(r0, CHUNK, CHUNK)
        kk = k_ref[0, rows, :].astype(F32)
        bc = _cumsum_rows(tril, lg_ref[0, rows, :])
        qf = q_ref[0, rows, :].astype(F32)
        vb = i_ref[0, rows, :]
        for blk in range(N_SUB):
            lo, hi = blk * SUB, (blk + 1) * SUB
            r = bc[lo - 1:lo, :] if blk > 0 else jnp.zeros((1, bc.shape[1]), F32)
            qt = qf[lo:hi] * jnp.exp(bc[lo:hi] - r)
            sub_rows = _ds(r0 + lo, SUB, SUB)
            qt_ref[sub_rows, :] = qt.astype(BF16)
            qh_ref[sub_rows, :] = (qt * jnp.exp(r)).astype(BF16)
            sec = _ds(s0 + SEC_OFF[blk], hi, SUB)
            kall_ref[sec, :] = (kk[:hi] * jnp.exp(jnp.minimum(r - bc[:hi], EXP_CLAMP))).astype(BF16)
            vall_ref[sec, :] = vb[:hi]
            worst = jnp.maximum(worst, r - bc[hi - 1:hi, :])
        b_last = bc[CHUNK - 1:CHUNK, :]
        kh_ref[rows, :] = (kk * jnp.exp(b_last - bc)).astype(BF16)
        dec_ref[_ds(c * 8, 8, 8), :] = jnp.broadcast_to(
            jnp.exp(b_last), (8, b_last.shape[1]))
        return worst

    worst = jnp.zeros((1, lg_ref.shape[2]), F32)
    for c in range(n_chunks):
        worst = prep(c, worst)

    trow = lax.broadcasted_iota(jnp.int32, (CHUNK, STACK), 0)
    col = lax.broadcasted_iota(jnp.int32, (CHUNK, STACK), 1)
    valid = None
    for blk in range(N_SUB):
        in_sec = (col >= SEC_OFF[blk]) & (col < SEC_OFF[blk] + SUB * (blk + 1))
        ok = in_sec & (trow >= blk * SUB) & (trow < (blk + 1) * SUB) & (col - SEC_OFF[blk] <= trow)
        valid = ok if valid is None else (valid | ok)

    def scan(c, carry):
        rows = _ds(c * CHUNK, CHUNK, CHUNK)
        srows = _ds(c * STACK, STACK, SUB)
        vrows = _ds(c * STACK + SEC_OFF[-1], CHUNK, SUB)
        drow = pl.ds(c * 8, 1)
        heads = [slice(h * A_EXPAND, (h + 1) * A_EXPAND) for h in range(A_HEADS)]
        sc = [_dot_nt(qt_ref[rows, cs], kall_ref[srows, cs]) for cs in heads]
        st = [st_ref[h] for h in range(A_HEADS)]
        inter = [_dot_nt(qh_ref[rows, cs], st[h].astype(BF16)) for h, cs in enumerate(heads)]
        upd = [_dot_tn(vall_ref[vrows, cs], kh_ref[rows, cs]) for cs in heads]
        out = [_dot(jnp.where(valid, sc[h], 0.0).astype(BF16), vall_ref[srows, cs]) + inter[h]
               for h, cs in enumerate(heads)]
        new_st = [st[h] * dec_ref[drow, cs] + upd[h] for h, cs in enumerate(heads)]
        for h, cs in enumerate(heads):
            oacc_ref[rows, cs] = out[h]
            st_ref[h] = new_st[h]
        return carry

    for c in range(n_chunks):
        scan(c, 0)

    def fin(c, carry):
        rows = _ds(c * CHUNK, CHUNK, CHUNK)
        for h in range(A_HEADS):
            cols = slice(h * A_EXPAND, (h + 1) * A_EXPAND)
            o = oacc_ref[rows, cols]
            y = o * lax.rsqrt(jnp.mean(o * o, axis=-1, keepdims=True) + RMS_EPS) * gn
            o_ref[0, rows, cols] = (y * g_ref[0, rows, cols].astype(F32)).astype(o_ref.dtype)
        return carry

    for c in range(n_chunks):
        fin(c, 0)

    @pl.when(jnp.max(worst) > EXP_CLAMP)
    def _():
        st_ref[...] = st0_ref[...]
        first_row = lax.broadcasted_iota(jnp.int32, (SUB, A_EXPAND), 0) == 0

        def group(g, carry):
            rows = _ds(g * SUB, SUB, SUB)
            f16 = jnp.exp(lg_ref[0, rows, :])
            k16 = k_ref[0, rows, :].astype(F32)
            q16 = q_ref[0, rows, :].astype(F32)
            v16 = i_ref[0, rows, :].astype(F32)
            for h in range(A_HEADS):
                cs = slice(h * A_EXPAND, (h + 1) * A_EXPAND)
                st = st_ref[h]
                outs = []
                for i in range(SUB):
                    one = lambda x: jnp.where(first_row, x[i:i + 1, cs], 0.0).astype(BF16)
                    st = st * f16[i:i + 1, cs] + _dot_tn(one(v16), one(k16))
                    outs.append(_dot_nt(one(q16), st.astype(BF16))[0:1, :])
                st_ref[h] = st
                oacc_ref[rows, cs] = jnp.concatenate(outs, axis=0)
            return carry

        lax.fori_loop(0, n_chunks * N_SUB, group, 0)
        for c in range(n_chunks):
            fin(c, 0)


def _hgrn(proj3, lg3, gnorm, t=256):
    bsz, seq, _ = proj3.shape
    nc = t // CHUNK
    blk = lambda off: pl.BlockSpec((1, t, D_MODEL), lambda b, s, o=off // D_MODEL: (b, s, o))
    return pl.pallas_call(
        functools.partial(_hgrn_kernel, n_chunks=nc),
        grid=(bsz, seq // t),
        in_specs=[blk(OFF_Q), blk(OFF_F), blk(OFF_I), blk(OFF_G), blk(0),
                  pl.BlockSpec((1, A_HEAD_I), lambda b, s: (0, 0))],
        out_specs=pl.BlockSpec((1, t, D_MODEL), lambda b, s: (b, s, 0)),
        out_shape=jax.ShapeDtypeStruct((bsz, seq, D_MODEL), BF16),
        scratch_shapes=[pltpu.VMEM((A_HEADS, A_HEAD_I, A_EXPAND), F32),
                        pltpu.VMEM((t, D_MODEL), BF16),
                        pltpu.VMEM((t, D_MODEL), BF16),
                        pltpu.VMEM((t, D_MODEL), BF16),
                        pltpu.VMEM((nc * STACK, D_MODEL), BF16),
                        pltpu.VMEM((nc * STACK, D_MODEL), BF16),
                        pltpu.VMEM((nc * 8, D_MODEL), F32),
                        pltpu.VMEM((t, D_MODEL), F32),
                        pltpu.VMEM((A_HEADS, A_HEAD_I, A_EXPAND), F32)],
        compiler_params=_cparams(("parallel", "arbitrary")),
        name="hgrn",
    )(proj3, proj3, proj3, proj3, lg3, gnorm.reshape(1, A_HEAD_I))


def _ssd_kernel(x_ref, bm_ref, cm_ref, z_ref, dt_ref, dtb_ref, alog_ref, dsk_ref, nw_ref,
                o_ref, xdt_ref, ax_ref, st_ref, *, n_chunks):
    grp = pl.program_id(1)

    @pl.when(pl.program_id(2) == 0)
    def _():
        st_ref[...] = jnp.zeros_like(st_ref)

    hrow = lax.broadcasted_iota(jnp.int32, (DT_PAD, B_GROUP_W), 0)
    hcol = lax.broadcasted_iota(jnp.int32, (DT_PAD, B_GROUP_W), 1) // B_HEADDIM + grp * B_HG
    expand = (hrow == hcol).astype(BF16)
    exp2 = jnp.concatenate([expand, expand], axis=0)

    def expand_row(ref):
        return _dot(_split_bf16(jnp.broadcast_to(ref[...], (8, DT_PAD))), exp2)[0:1, :]

    dtv = dt_ref[0] + dtb_ref[...]
    delta = jnp.maximum(dtv, 0.0) + jnp.log1p(jnp.exp(-jnp.abs(dtv)))
    delta_x = _dot(_split_bf16(delta), exp2)
    ax_ref[...] = delta_x * (-jnp.exp(expand_row(alog_ref)))
    xdt_ref[...] = x_ref[0].astype(F32) * delta_x

    tril = _tril_bf16(CHUNK)
    lane = lax.broadcasted_iota(jnp.int32, (CHUNK, 2 * B_HEADDIM), 1)
    trow = lax.broadcasted_iota(jnp.int32, (CHUNK, 2 * B_HEADDIM), 0)
    left = lane < B_HEADDIM
    spos = jnp.where(left, lane, lane - B_HEADDIM)
    causal2 = trow >= spos
    n_pairs = B_GROUP_W // (2 * B_HEADDIM)
    keep = jnp.tile((trow <= spos).astype(BF16), (1, 2 * n_pairs))
    ones8 = jnp.ones((8, CHUNK), BF16)

    dsk = expand_row(dsk_ref)

    def one_chunk(c, st):
        rows = _ds(c * CHUNK, CHUNK, CHUNK)
        bm = bm_ref[0, rows, :]
        cm = cm_ref[0, rows, :]
        xdt = xdt_ref[rows, :]
        axs = _split_bf16(ax_ref[rows, :])
        acx2 = _dot(tril, axs)
        acx = acx2[:, :B_GROUP_W] + acx2[:, B_GROUP_W:]
        a_last = acx[CHUNK - 1:CHUNK, :]
        kr2 = _dot(ones8, axs * keep)
        krow = kr2[0:1, :B_GROUP_W] + kr2[0:1, B_GROUP_W:]
        seg = acx - krow

        cb2 = _dot_nt(cm, jnp.concatenate([bm, bm], axis=0))
        inter = _dot(cm, st.astype(BF16)) * jnp.exp(acx)
        xw = (xdt * jnp.exp(a_last - acx)).astype(BF16)
        new_st = st * jnp.exp(a_last) + _dot_tn(bm, xw)
        y_tiles = []
        for p in range(n_pairs):
            cs = slice(p * 2 * B_HEADDIM, (p + 1) * 2 * B_HEADDIM)
            decay = jnp.exp(jnp.where(causal2, seg[:, cs], -jnp.inf))
            m = (cb2 * decay).astype(BF16)
            xp = xdt[:, cs]
            bd = jnp.concatenate([jnp.where(left, xp, 0.0), jnp.where(left, 0.0, xp)], axis=0)
            y_tiles.append(_dot(m, bd.astype(BF16)) + inter[:, cs])
        y = jnp.concatenate(y_tiles, axis=1)
        y = (y + x_ref[0, rows, :].astype(F32) * dsk) * z_ref[0, rows, :].astype(F32)
        y = y * lax.rsqrt(jnp.mean(y * y, axis=-1, keepdims=True) + RMS_EPS) * nw_ref[...]
        return y.astype(o_ref.dtype), new_st

    def body(i, carry):
        st = st_ref[...]
        outs = []
        for k in range(SSD_CHUNKS_PER_STEP):
            y, st = one_chunk(i * SSD_CHUNKS_PER_STEP + k, st)
            outs.append(y)
        for k, y in enumerate(outs):
            r0 = (i * SSD_CHUNKS_PER_STEP + k) * CHUNK
            o_ref[0, pl.ds(pl.multiple_of(r0, CHUNK), CHUNK), :] = y
        st_ref[...] = st
        return carry

    lax.fori_loop(0, n_chunks // SSD_CHUNKS_PER_STEP, body, 0)


def _ssd(proj3, dt3, dt_bias, a_log, d_skip, w_norm, t=512):
    bsz, seq, _ = proj3.shape
    nc = t // CHUNK
    gw, ns = B_GROUP_W, B_STATE
    pad32 = lambda v: jnp.pad(v.reshape(1, B_HEADS), ((0, 0), (0, DT_PAD - B_HEADS)))
    act = lambda width, off: pl.BlockSpec(
        (1, t, width), lambda b, g, s, o=off // width: (b, s, o + g))
    par = lambda rows, width, off: pl.BlockSpec(
        (rows, width), lambda b, g, s, o=off // width: (0, o + g))
    full = lambda shape: pl.BlockSpec(shape, lambda b, g, s: (0, 0))
    return pl.pallas_call(
        functools.partial(_ssd_kernel, n_chunks=nc),
        grid=(bsz, B_GROUPS, seq // t),
        in_specs=[act(gw, OFF_X), act(ns, OFF_B), act(ns, OFF_C), act(gw, OFF_Z),
                  pl.BlockSpec((1, t, DT_PAD), lambda b, g, s: (b, s, 0)),
                  full((1, DT_PAD)), full((1, DT_PAD)), full((1, DT_PAD)),
                  par(1, gw, 0)],
        out_specs=pl.BlockSpec((1, t, gw), lambda b, g, s: (b, s, g)),
        out_shape=jax.ShapeDtypeStruct((bsz, seq, B_INNER), BF16),
        scratch_shapes=[pltpu.VMEM((t, gw), F32),
                        pltpu.VMEM((t, gw), F32),
                        pltpu.VMEM((ns, gw), F32)],
        compiler_params=_cparams(("parallel", "parallel", "arbitrary")),
        name="ssd",
    )(proj3, proj3, proj3, proj3, dt3,
      pad32(dt_bias), pad32(a_log), pad32(d_skip), w_norm.reshape(1, B_INNER))


MERGE_ROWS = 256


def _merge_kernel(ya_ref, yb_ref, ga_ref, gb_ref, x_ref, mod_ref, wa_ref, wb_ref, wo_ref,
                  lg_ref, lbias_ref, o_ref):
    m = mod_ref[0]
    for r in range(0, x_ref.shape[0], MERGE_ROWS):
        rs = slice(r, r + MERGE_ROWS)
        pa = _dot(ya_ref[rs, :], wa_ref[...])
        pb = _dot(yb_ref[rs, :], wb_ref[...])
        merged = ga_ref[rs, :].astype(F32) * pa + gb_ref[rs, :].astype(F32) * pb
        h = _dot(merged.astype(BF16), wo_ref[...])
        y = _layer_norm(ALPHA * x_ref[rs, :] + m[2:3, :] * h)
        o_ref[rs, :] = y * lg_ref[...] + lbias_ref[...]


def _merge(ya2, yb2, proj2, x2, mod3, w_a, w_b, w_o, ln_g, ln_b, seq):
    n_tok, d = x2.shape
    tm = 1024
    tiles_per_batch = seq // tm
    row = lambda width, off=0: pl.BlockSpec((tm, width), lambda i, o=off // width: (i, o))
    full = lambda shape: pl.BlockSpec(shape, lambda i: (0, 0), pipeline_mode=pl.Buffered(1))
    return pl.pallas_call(
        _merge_kernel,
        grid=(n_tok // tm,),
        in_specs=[row(d), row(B_INNER), row(d, OFF_GA), row(d, OFF_GB), row(d),
                  pl.BlockSpec((1, 6, d), lambda i: (i // tiles_per_batch, 0, 0)),
                  full((d, d)), full((B_INNER, d)), full((d, d)),
                  full((1, d)), full((1, d))],
        out_specs=row(d),
        out_shape=jax.ShapeDtypeStruct((n_tok, d), F32),
        compiler_params=_cparams(("parallel",)),
        name="merge",
    )(ya2, yb2, proj2, proj2, x2, mod3, w_a, w_b, w_o, ln_g.reshape(1, d), ln_b.reshape(1, d))


FFN_ROWS = 256


def _ffn_kernel(x_ref, mod_ref, wg_ref, wu_ref, wd_ref, lg_ref, lbias_ref, o_ref):
    m = mod_ref[0]
    for r in range(0, x_ref.shape[0], FFN_ROWS):
        rs = slice(r, r + FFN_ROWS)
        x = x_ref[rs, :]
        u = (_layer_norm(x) * (1.0 + m[4:5, :]) + m[3:4, :]).astype(BF16)
        hidden = _silu(_dot(u, wg_ref[...])) * _dot(u, wu_ref[...])
        h = _dot(hidden.astype(BF16), wd_ref[...])
        y = _layer_norm(ALPHA * x + m[5:6, :] * h)
        o_ref[rs, :] = y * lg_ref[...] + lbias_ref[...]


def _ffn(x2, mod3, w_gate, w_up, w_down, ln_g, ln_b, seq):
    n_tok, d = x2.shape
    tm = 1024
    tiles_per_batch = seq // tm
    once = lambda shape: pl.BlockSpec(shape, lambda i: (0, 0), pipeline_mode=pl.Buffered(1))
    return pl.pallas_call(
        _ffn_kernel,
        grid=(n_tok // tm,),
        in_specs=[pl.BlockSpec((tm, d), lambda i: (i, 0)),
                  pl.BlockSpec((1, 6, d), lambda i: (i // tiles_per_batch, 0, 0)),
                  once((d, D_FF)), once((d, D_FF)), once((D_FF, d)),
                  once((1, d)), once((1, d))],
        out_specs=pl.BlockSpec((tm, d), lambda i: (i, 0)),
        out_shape=jax.ShapeDtypeStruct((n_tok, d), F32),
        compiler_params=_cparams(("parallel",)),
        name="ffn",
    )(x2, mod3, w_gate, w_up, w_down, ln_g.reshape(1, d), ln_b.reshape(1, d))


def kernel(x, c, w_ada, b_ada, w_in, hgrn_lb, hgrn_gnorm, ssm_conv_w, ssm_conv_b, ssm_dt_bias,
           ssm_a_log, ssm_d, ssm_norm, w_branch_a, w_branch_b, w_o, ln1_g, ln1_b,
           w_ffn_gate, w_ffn_up, w_ffn_down, ln2_g, ln2_b):
    bsz, seq, d = x.shape
    assert d == D_MODEL and w_ada.shape[0] == DEPTH == 1
    x2 = x.reshape(bsz * seq, d)
    for l in range(DEPTH):
        mod3 = _ada(c, w_ada[l], b_ada[l]).reshape(bsz, 6, d)
        w = w_in[l]
        dt_lo = OFF_X + B_INNER + 2 * B_GROUPS * B_STATE
        w_main = jnp.concatenate([w[:, :dt_lo], w[:, dt_lo + B_HEADS:]], axis=1).astype(BF16)
        w_dt = jnp.pad(w[:, dt_lo:dt_lo + B_HEADS], ((0, 0), (0, DT_PAD - B_HEADS))).astype(BF16)
        proj2, lg2, dt2 = _in_proj(x2, mod3, w_main, w_dt, hgrn_lb, ssm_conv_w[l], ssm_conv_b[l], seq)
        proj3 = proj2.reshape(bsz, seq, MAIN_DIM)
        ya = _hgrn(proj3, lg2.reshape(bsz, seq, D_MODEL), hgrn_gnorm[l])
        yb = _ssd(proj3, dt2.reshape(bsz, seq, DT_PAD),
                  ssm_dt_bias[l], ssm_a_log[l], ssm_d[l], ssm_norm[l])
        x2 = _merge(ya.reshape(bsz * seq, d), yb.reshape(bsz * seq, B_INNER), proj2, x2, mod3,
                    w_branch_a[l].astype(BF16), w_branch_b[l].astype(BF16), w_o[l].astype(BF16),
                    ln1_g[l], ln1_b[l], seq)
        x2 = _ffn(x2, mod3, w_ffn_gate[l].astype(BF16), w_ffn_up[l].astype(BF16),
                  w_ffn_down[l].astype(BF16), ln2_g[l], ln2_b[l], seq)
    return x2.reshape(bsz, seq, d)
```

```python
import functools

import jax
import jax.numpy as jnp
from jax import lax
from jax.experimental import pallas as pl
from jax.experimental.pallas import tpu as pltpu

F32 = jnp.float32
BF16 = jnp.bfloat16

D_MODEL = 1024
DEPTH = 1
A_HEADS = 8
A_EXPAND = 128
A_HEAD_I = 128
B_INNER = 2048
B_HEADDIM = 64
B_HEADS = 32
B_GROUPS = 4
B_HG = 8
B_STATE = 128
B_CONV = 4
B_GROUP_W = B_INNER // B_GROUPS
D_FF = 2816
ALPHA = (2.0 * DEPTH) ** 0.25
LN_EPS = 1e-5
RMS_EPS = 1e-6

OFF_Q, OFF_F, OFF_I, OFF_G = 0, 1024, 2048, 3072
OFF_Z = 4096
OFF_X = 6144
OFF_B = OFF_X + B_INNER
OFF_C = OFF_B + B_GROUPS * B_STATE
OFF_GA = 9216
OFF_GB = 10240
MAIN_DIM = 11264
DT_PAD = 128

CHUNK = 64
SUB = 16
N_SUB = CHUNK // SUB
SEC_OFF = tuple(SUB * i * (i + 1) // 2 for i in range(N_SUB))
STACK = SUB * N_SUB * (N_SUB + 1) // 2
EXP_CLAMP = 80.0
SSD_CHUNKS_PER_STEP = 8

VMEM_LIMIT = 56 * 1024 * 1024


def _cparams(sem):
    return pltpu.CompilerParams(dimension_semantics=sem, vmem_limit_bytes=VMEM_LIMIT)


def _layer_norm(x):
    mu = jnp.mean(x, axis=-1, keepdims=True)
    xc = x - mu
    return xc * lax.rsqrt(jnp.mean(xc * xc, axis=-1, keepdims=True) + LN_EPS)


def _silu(x):
    return x * jax.nn.sigmoid(x)


def _split_bf16(x, axis=-1):
    hi = lax.bitcast_convert_type(
        lax.bitcast_convert_type(x, jnp.uint32) & jnp.uint32(0xFFFF0000), F32)
    return jnp.concatenate([hi.astype(BF16), (x - hi).astype(BF16)], axis=axis)


def _dot(a, b):
    return jnp.dot(a, b, preferred_element_type=F32)


def _dot_nt(a, b):
    return lax.dot_general(a, b, (((1,), (1,)), ((), ())), preferred_element_type=F32)


def _dot_tn(a, b):
    return lax.dot_general(a, b, (((0,), (0,)), ((), ())), preferred_element_type=F32)


def _ds(start, size, align):
    if isinstance(start, int):
        return pl.ds(start, size)
    return pl.ds(pl.multiple_of(start, align), size)


def _tril_bf16(n):
    return (lax.broadcasted_iota(jnp.int32, (n, n), 0)
            >= lax.broadcasted_iota(jnp.int32, (n, n), 1)).astype(BF16)


def _cumsum_rows(tril, x):
    w = x.shape[1]
    y = _dot(tril, _split_bf16(x))
    return y[:, :w] + y[:, w:]


def _ada_kernel(c_ref, w_ref, b_ref, o_ref):
    cond = _silu(c_ref[...])
    o_ref[...] = jnp.dot(cond, w_ref[...], preferred_element_type=F32,
                         precision=lax.Precision.HIGHEST) + b_ref[...]


def _ada(c, w_ada, b_ada):
    bsz, d = c.shape
    n = w_ada.shape[1]
    tn = 1536
    return pl.pallas_call(
        _ada_kernel,
        grid=(n // tn,),
        in_specs=[pl.BlockSpec((bsz, d), lambda j: (0, 0)),
                  pl.BlockSpec((d, tn), lambda j: (0, j)),
                  pl.BlockSpec((1, tn), lambda j: (0, j))],
        out_specs=pl.BlockSpec((bsz, tn), lambda j: (0, j)),
        out_shape=jax.ShapeDtypeStruct((bsz, n), F32),
        compiler_params=_cparams(("arbitrary",)),
        name="ada",
    )(c, w_ada, b_ada.reshape(1, n))


TILE_Q, TILE_F, TILE_G, TILE_X0, TILE_GATE0 = 0, 1, 3, 6, 9
INPROJ_ROWS = 128
LANES = 128
CONV_STRIDE = 4


def _inproj_tile(j):
    return jnp.where(j < 2, 1 - j, j)


def _inproj_kernel(x_ref, mod_ref, w_ref, wdt_ref, lb_ref, cw_ref, cb_ref,
                   o_ref, lg_ref, dt_ref, u_ref, cpad_ref, ctmp_ref, *, tiles_per_batch):
    j = pl.program_id(1)
    tm = x_ref.shape[0]
    pieces = [slice(r, r + INPROJ_ROWS) for r in range(0, tm, INPROJ_ROWS)]

    @pl.when(j == 0)
    def _():
        m = mod_ref[0]
        lbp = lb_ref[...]
        lbe = jnp.exp(lbp - jnp.max(lbp, axis=0, keepdims=True))
        lb = lbe[0:1, :] / jnp.sum(lbe, axis=0, keepdims=True)
        for rs in pieces:
            u = _layer_norm(x_ref[rs, :]) * (1.0 + m[1:2, :]) + m[0:1, :]
            ub = u.astype(BF16)
            u_ref[rs, :] = ub
            dt_ref[rs, :] = _dot(ub, wdt_ref[...])
            f = lb + (1.0 - lb) * jax.nn.sigmoid(_dot(ub, w_ref[...]))
            lg_ref[rs, :] = jnp.log(f)
            o_ref[rs, :] = (1.0 - f).astype(o_ref.dtype)

    t = _inproj_tile(j)
    is_q = t == TILE_Q
    is_silu = is_q | ((t >= TILE_G) & (t < TILE_X0))
    is_gate = t >= TILE_GATE0

    @pl.when(is_silu)
    def _():
        scale = jnp.where(is_q, A_EXPAND ** -0.5, 1.0)
        for rs in pieces:
            acc = _dot(u_ref[rs, :], w_ref[...])
            o_ref[rs, :] = (acc * (jax.nn.sigmoid(acc) * scale)).astype(o_ref.dtype)

    @pl.when(is_gate)
    def _():
        for rs in pieces:
            o_ref[rs, :] = jax.nn.sigmoid(_dot(u_ref[rs, :], w_ref[...])).astype(o_ref.dtype)

    is_conv = (t >= TILE_X0) & (t < TILE_GATE0)
    seq_start = pl.program_id(0) % tiles_per_batch == 0
    n_slabs = w_ref.shape[1] // LANES

    def conv_tile(slab0):
        @pl.when(seq_start)
        def _():
            for s in range(n_slabs):
                cpad_ref[slab0 + s, 0:8, :] = jnp.zeros((8, LANES), F32)

        @pl.when(jnp.logical_not(seq_start))
        def _():
            for s in range(n_slabs):
                cpad_ref[slab0 + s, 0:8, :] = cpad_ref[slab0 + s, tm:tm + 8, :]

        n_str = INPROJ_ROWS // CONV_STRIDE
        for p, rs in enumerate(pieces):
            lo = 8 + rs.start
            acc = _dot(u_ref[rs, :], w_ref[...])
            for s in range(n_slabs):
                cpad_ref[slab0 + s, lo:lo + INPROJ_ROWS, :] = acc[:, s * LANES:(s + 1) * LANES]
            for s in range(n_slabs):
                ls = slice(s * LANES, (s + 1) * LANES)
                pad = cpad_ref.at[slab0 + s]
                out = ctmp_ref.at[p * n_slabs + s]
                for ph in range(CONV_STRIDE):
                    y = cb_ref[:, ls]
                    for k in range(B_CONV):
                        start = lo + ph - (B_CONV - 1) + k
                        y = y + cw_ref[k:k + 1, ls] * pad[pl.ds(start, n_str, stride=CONV_STRIDE), :]
                    out[pl.ds(ph, n_str, stride=CONV_STRIDE), :] = _silu(y)
            o_ref[rs, :] = jnp.concatenate(
                [ctmp_ref[p * n_slabs + s] for s in range(n_slabs)], axis=1).astype(o_ref.dtype)

    for kt in range(TILE_GATE0 - TILE_X0):
        pl.when(t == TILE_X0 + kt)(functools.partial(conv_tile, kt * n_slabs))

    @pl.when((j > 0) & jnp.logical_not(is_silu | is_gate | is_conv))
    def _():
        o_ref[...] = _dot(u_ref[...], w_ref[...]).astype(o_ref.dtype)


def _in_proj(x2, mod3, w_main, w_dt, hgrn_lb, conv_w, conv_b, seq):
    n_tok, d = x2.shape
    tm, tn = 1024, D_MODEL
    tiles_per_batch = seq // tm
    n_conv = TILE_GATE0 - TILE_X0
    conv_tile = lambda j: jnp.clip(_inproj_tile(j) - TILE_X0, 0, n_conv - 1)
    return pl.pallas_call(
        functools.partial(_inproj_kernel, tiles_per_batch=tiles_per_batch),
        grid=(n_tok // tm, MAIN_DIM // tn),
        in_specs=[pl.BlockSpec((tm, d), lambda i, j: (i, 0)),
                  pl.BlockSpec((1, 6, d), lambda i, j: (i // tiles_per_batch, 0, 0)),
                  pl.BlockSpec((d, tn), lambda i, j: (0, _inproj_tile(j))),
                  pl.BlockSpec((d, DT_PAD), lambda i, j: (0, 0)),
                  pl.BlockSpec(hgrn_lb.shape, lambda i, j: (0, 0)),
                  pl.BlockSpec((B_CONV, tn), lambda i, j: (0, conv_tile(j))),
                  pl.BlockSpec((1, tn), lambda i, j: (0, conv_tile(j)))],
        out_specs=[pl.BlockSpec((tm, tn), lambda i, j: (i, _inproj_tile(j))),
                   pl.BlockSpec((tm, tn), lambda i, j: (i, 0)),
                   pl.BlockSpec((tm, DT_PAD), lambda i, j: (i, 0))],
        out_shape=[jax.ShapeDtypeStruct((n_tok, MAIN_DIM), BF16),
                   jax.ShapeDtypeStruct((n_tok, tn), F32),
                   jax.ShapeDtypeStruct((n_tok, DT_PAD), F32)],
        scratch_shapes=[pltpu.VMEM((tm, d), BF16),
                        pltpu.VMEM((n_conv * (tn // LANES), tm + 8, LANES), F32),
                        pltpu.VMEM(((tm // INPROJ_ROWS) * (tn // LANES), INPROJ_ROWS, LANES),
                                   F32)],
        compiler_params=_cparams(("arbitrary", "arbitrary")),
        name="in_proj",
    )(x2, mod3, w_main, w_dt, hgrn_lb, conv_w, conv_b.reshape(1, -1))


def _hgrn_kernel(q_ref, k_ref, i_ref, g_ref, lg_ref, gn_ref, o_ref,
                 st_ref, qt_ref, qh_ref, kh_ref, kall_ref, vall_ref, dec_ref, oacc_ref, st0_ref,
                 *, n_chunks):
    @pl.when(pl.program_id(1) == 0)
    def _():
        st_ref[...] = jnp.zeros_like(st_ref)

    st0_ref[...] = st_ref[...]
    gn = gn_ref[...]
    tril = _tril_bf16(CHUNK)

    def prep(c, worst):
        r0 = c * CHUNK
        s0 = c * STACK
        rows = _ds(r0, CHUNK, CHUNK)
        kk = k_ref[0, rows, :].astype(F32)
        bc = _cumsum_rows(tril, lg_ref[0, rows, :])
        qf = q_ref[0, rows, :].astype(F32)
        vb = i_ref[0, rows, :]
        for blk in range(N_SUB):
            lo, hi = blk * SUB, (blk + 1) * SUB
            r = bc[lo - 1:lo, :] if blk > 0 else jnp.zeros((1, bc.shape[1]), F32)
            qt = qf[lo:hi] * jnp.exp(bc[lo:hi] - r)
            sub_rows = _ds(r0 + lo, SUB, SUB)
            qt_ref[sub_rows, :] = qt.astype(BF16)
            qh_ref[sub_rows, :] = (qt * jnp.exp(r)).astype(BF16)
            sec = _ds(s0 + SEC_OFF[blk], hi, SUB)
            kall_ref[sec, :] = (kk[:hi] * jnp.exp(jnp.minimum(r - bc[:hi], EXP_CLAMP))).astype(BF16)
            vall_ref[sec, :] = vb[:hi]
            worst = jnp.maximum(worst, r - bc[hi - 1:hi, :])
        b_last = bc[CHUNK - 1:CHUNK, :]
        kh_ref[rows, :] = (kk * jnp.exp(b_last - bc)).astype(BF16)
        dec_ref[_ds(c * 8, 8, 8), :] = jnp.broadcast_to(
            jnp.exp(b_last), (8, b_last.shape[1]))
        return worst

    worst = jnp.zeros((1, lg_ref.shape[2]), F32)
    for c in range(n_chunks):
        worst = prep(c, worst)

    trow = lax.broadcasted_iota(jnp.int32, (CHUNK, STACK), 0)
    col = lax.broadcasted_iota(jnp.int32, (CHUNK, STACK), 1)
    valid = None
    for blk in range(N_SUB):
        in_sec = (col >= SEC_OFF[blk]) & (col < SEC_OFF[blk] + SUB * (blk + 1))
        ok = in_sec & (trow >= blk * SUB) & (trow < (blk + 1) * SUB) & (col - SEC_OFF[blk] <= trow)
        valid = ok if valid is None else (valid | ok)

    def scan(c, carry):
        rows = _ds(c * CHUNK, CHUNK, CHUNK)
        srows = _ds(c * STACK, STACK, SUB)
        vrows = _ds(c * STACK + SEC_OFF[-1], CHUNK, SUB)
        drow = pl.ds(c * 8, 1)
        heads = [slice(h * A_EXPAND, (h + 1) * A_EXPAND) for h in range(A_HEADS)]
        sc = [_dot_nt(qt_ref[rows, cs], kall_ref[srows, cs]) for cs in heads]
        st = [st_ref[h] for h in range(A_HEADS)]
        inter = [_dot_nt(qh_ref[rows, cs], st[h].astype(BF16)) for h, cs in enumerate(heads)]
        upd = [_dot_tn(vall_ref[vrows, cs], kh_ref[rows, cs]) for cs in heads]
        out = [_dot(jnp.where(valid, sc[h], 0.0).astype(BF16), vall_ref[srows, cs]) + inter[h]
               for h, cs in enumerate(heads)]
        new_st = [st[h] * dec_ref[drow, cs] + upd[h] for h, cs in enumerate(heads)]
        for h, cs in enumerate(heads):
            oacc_ref[rows, cs] = out[h]
            st_ref[h] = new_st[h]
        return carry

    for c in range(n_chunks):
        scan(c, 0)

    def fin(c, carry):
        rows = _ds(c * CHUNK, CHUNK, CHUNK)
        for h in range(A_HEADS):
            cols = slice(h * A_EXPAND, (h + 1) * A_EXPAND)
            o = oacc_ref[rows, cols]
            y = o * lax.rsqrt(jnp.mean(o * o, axis=-1, keepdims=True) + RMS_EPS) * gn
            o_ref[0, rows, cols] = (y * g_ref[0, rows, cols].astype(F32)).astype(o_ref.dtype)
        return carry

    for c in range(n_chunks):
        fin(c, 0)

    @pl.when(jnp.max(worst) > EXP_CLAMP)
    def _():
        st_ref[...] = st0_ref[...]
        first_row = lax.broadcasted_iota(jnp.int32, (SUB, A_EXPAND), 0) == 0

        def group(g, carry):
            rows = _ds(g * SUB, SUB, SUB)
            f16 = jnp.exp(lg_ref[0, rows, :])
            k16 = k_ref[0, rows, :].astype(F32)
            q16 = q_ref[0, rows, :].astype(F32)
            v16 = i_ref[0, rows, :].astype(F32)
            for h in range(A_HEADS):
                cs = slice(h * A_EXPAND, (h + 1) * A_EXPAND)
                st = st_ref[h]
                outs = []
                for i in range(SUB):
                    one = lambda x: jnp.where(first_row, x[i:i + 1, cs], 0.0).astype(BF16)
                    st = st * f16[i:i + 1, cs] + _dot_tn(one(v16), one(k16))
                    outs.append(_dot_nt(one(q16), st.astype(BF16))[0:1, :])
                st_ref[h] = st
                oacc_ref[rows, cs] = jnp.concatenate(outs, axis=0)
            return carry

        lax.fori_loop(0, n_chunks * N_SUB, group, 0)
        for c in range(n_chunks):
            fin(c, 0)


def _hgrn(proj3, lg3, gnorm, t=512):
    bsz, seq, _ = proj3.shape
    nc = t // CHUNK
    blk = lambda off: pl.BlockSpec((1, t, D_MODEL), lambda b, s, o=off // D_MODEL: (b, s, o))
    return pl.pallas_call(
        functools.partial(_hgrn_kernel, n_chunks=nc),
        grid=(bsz, seq // t),
        in_specs=[blk(OFF_Q), blk(OFF_F), blk(OFF_I), blk(OFF_G), blk(0),
                  pl.BlockSpec((1, A_HEAD_I), lambda b, s: (0, 0))],
        out_specs=pl.BlockSpec((1, t, D_MODEL), lambda b, s: (b, s, 0)),
        out_shape=jax.ShapeDtypeStruct((bsz, seq, D_MODEL), BF16),
        scratch_shapes=[pltpu.VMEM((A_HEADS, A_HEAD_I, A_EXPAND), F32),
                        pltpu.VMEM((t, D_MODEL), BF16),
                        pltpu.VMEM((t, D_MODEL), BF16),
                        pltpu.VMEM((t, D_MODEL), BF16),
                        pltpu.VMEM((nc * STACK, D_MODEL), BF16),
                        pltpu.VMEM((nc * STACK, D_MODEL), BF16),
                        pltpu.VMEM((nc * 8, D_MODEL), F32),
                        pltpu.VMEM((t, D_MODEL), F32),
                        pltpu.VMEM((A_HEADS, A_HEAD_I, A_EXPAND), F32)],
        compiler_params=_cparams(("parallel", "arbitrary")),
        name="hgrn",
    )(proj3, proj3, proj3, proj3, lg3, gnorm.reshape(1, A_HEAD_I))


def _ssd_kernel(x_ref, bm_ref, cm_ref, z_ref, dt_ref, dtb_ref, alog_ref, dsk_ref, nw_ref,
                o_ref, xdt_ref, ax_ref, st_ref, *, n_chunks):
    grp = pl.program_id(1)

    @pl.when(pl.program_id(2) == 0)
    def _():
        st_ref[...] = jnp.zeros_like(st_ref)

    hrow = lax.broadcasted_iota(jnp.int32, (DT_PAD, B_GROUP_W), 0)
    hcol = lax.broadcasted_iota(jnp.int32, (DT_PAD, B_GROUP_W), 1) // B_HEADDIM + grp * B_HG
    expand = (hrow == hcol).astype(BF16)
    exp2 = jnp.concatenate([expand, expand], axis=0)

    def expand_row(ref):
        return _dot(_split_bf16(jnp.broadcast_to(ref[...], (8, DT_PAD))), exp2)[0:1, :]

    dtv = dt_ref[0] + dtb_ref[...]
    delta = jnp.maximum(dtv, 0.0) + jnp.log1p(jnp.exp(-jnp.abs(dtv)))
    delta_x = _dot(_split_bf16(delta), exp2)
    ax_ref[...] = delta_x * (-jnp.exp(expand_row(alog_ref)))
    xdt_ref[...] = x_ref[0].astype(F32) * delta_x

    tril = _tril_bf16(CHUNK)
    lane = lax.broadcasted_iota(jnp.int32, (CHUNK, 2 * B_HEADDIM), 1)
    trow = lax.broadcasted_iota(jnp.int32, (CHUNK, 2 * B_HEADDIM), 0)
    left = lane < B_HEADDIM
    spos = jnp.where(left, lane, lane - B_HEADDIM)
    causal2 = trow >= spos
    n_pairs = B_GROUP_W // (2 * B_HEADDIM)
    keep = jnp.tile((trow <= spos).astype(BF16), (1, 2 * n_pairs))
    ones8 = jnp.ones((8, CHUNK), BF16)

    dsk = expand_row(dsk_ref)

    def one_chunk(c, st):
        rows = _ds(c * CHUNK, CHUNK, CHUNK)
        bm = bm_ref[0, rows, :]
        cm = cm_ref[0, rows, :]
        xdt = xdt_ref[rows, :]
        axs = _split_bf16(ax_ref[rows, :])
        acx2 = _dot(tril, axs)
        acx = acx2[:, :B_GROUP_W] + acx2[:, B_GROUP_W:]
        a_last = acx[CHUNK - 1:CHUNK, :]
        kr2 = _dot(ones8, axs * keep)
        krow = kr2[0:1, :B_GROUP_W] + kr2[0:1, B_GROUP_W:]
        seg = acx - krow

        cb2 = _dot_nt(cm, jnp.concatenate([bm, bm], axis=0))
        inter = _dot(cm, st.astype(BF16)) * jnp.exp(acx)
        xw = (xdt * jnp.exp(a_last - acx)).astype(BF16)
        new_st = st * jnp.exp(a_last) + _dot_tn(bm, xw)
        y_tiles = []
        for p in range(n_pairs):
            cs = slice(p * 2 * B_HEADDIM, (p + 1) * 2 * B_HEADDIM)
            decay = jnp.exp(jnp.where(causal2, seg[:, cs], -jnp.inf))
            m = (cb2 * decay).astype(BF16)
            xp = xdt[:, cs]
            bd = jnp.concatenate([jnp.where(left, xp, 0.0), jnp.where(left, 0.0, xp)], axis=0)
            y_tiles.append(_dot(m, bd.astype(BF16)) + inter[:, cs])
        y = jnp.concatenate(y_tiles, axis=1)
        y = (y + x_ref[0, rows, :].astype(F32) * dsk) * z_ref[0, rows, :].astype(F32)
        y = y * lax.rsqrt(jnp.mean(y * y, axis=-1, keepdims=True) + RMS_EPS) * nw_ref[...]
        return y.astype(o_ref.dtype), new_st

    def body(i, carry):
        st = st_ref[...]
        outs = []
        for k in range(SSD_CHUNKS_PER_STEP):
            y, st = one_chunk(i * SSD_CHUNKS_PER_STEP + k, st)
            outs.append(y)
        for k, y in enumerate(outs):
            r0 = (i * SSD_CHUNKS_PER_STEP + k) * CHUNK
            o_ref[0, pl.ds(pl.multiple_of(r0, CHUNK), CHUNK), :] = y
        st_ref[...] = st
        return carry

    lax.fori_loop(0, n_chunks // SSD_CHUNKS_PER_STEP, body, 0)


def _ssd(proj3, dt3, dt_bias, a_log, d_skip, w_norm, t=512):
    bsz, seq, _ = proj3.shape
    nc = t // CHUNK
    gw, ns = B_GROUP_W, B_STATE
    pad32 = lambda v: jnp.pad(v.reshape(1, B_HEADS), ((0, 0), (0, DT_PAD - B_HEADS)))
    act = lambda width, off: pl.BlockSpec(
        (1, t, width), lambda b, g, s, o=off // width: (b, s, o + g))
    par = lambda rows, width, off: pl.BlockSpec(
        (rows, width), lambda b, g, s, o=off // width: (0, o + g))
    full = lambda shape: pl.BlockSpec(shape, lambda b, g, s: (0, 0))
    return pl.pallas_call(
        functools.partial(_ssd_kernel, n_chunks=nc),
        grid=(bsz, B_GROUPS, seq // t),
        in_specs=[act(gw, OFF_X), act(ns, OFF_B), act(ns, OFF_C), act(gw, OFF_Z),
                  pl.BlockSpec((1, t, DT_PAD), lambda b, g, s: (b, s, 0)),
                  full((1, DT_PAD)), full((1, DT_PAD)), full((1, DT_PAD)),
                  par(1, gw, 0)],
        out_specs=pl.BlockSpec((1, t, gw), lambda b, g, s: (b, s, g)),
        out_shape=jax.ShapeDtypeStruct((bsz, seq, B_INNER), BF16),
        scratch_shapes=[pltpu.VMEM((t, gw), F32),
                        pltpu.VMEM((t, gw), F32),
                        pltpu.VMEM((ns, gw), F32)],
        compiler_params=_cparams(("parallel", "parallel", "arbitrary")),
        name="ssd",
    )(proj3, proj3, proj3, proj3, dt3,
      pad32(dt_bias), pad32(a_log), pad32(d_skip), w_norm.reshape(1, B_INNER))


MERGE_ROWS = 256


def _merge_kernel(ya_ref, yb_ref, ga_ref, gb_ref, x_ref, mod_ref, wa_ref, wb_ref, wo_ref,
                  lg_ref, lbias_ref, o_ref):
    m = mod_ref[0]
    for r in range(0, x_ref.shape[0], MERGE_ROWS):
        rs = slice(r, r + MERGE_ROWS)
        pa = _dot(ya_ref[rs, :], wa_ref[...])
        pb = _dot(yb_ref[rs, :], wb_ref[...])
        merged = ga_ref[rs, :].astype(F32) * pa + gb_ref[rs, :].astype(F32) * pb
        h = _dot(merged.astype(BF16), wo_ref[...])
        y = _layer_norm(ALPHA * x_ref[rs, :] + m[2:3, :] * h)
        o_ref[rs, :] = y * lg_ref[...] + lbias_ref[...]


def _merge(ya2, yb2, proj2, x2, mod3, w_a, w_b, w_o, ln_g, ln_b, seq):
    n_tok, d = x2.shape
    tm = 1024
    tiles_per_batch = seq // tm
    row = lambda width, off=0: pl.BlockSpec((tm, width), lambda i, o=off // width: (i, o))
    full = lambda shape: pl.BlockSpec(shape, lambda i: (0, 0), pipeline_mode=pl.Buffered(1))
    return pl.pallas_call(
        _merge_kernel,
        grid=(n_tok // tm,),
        in_specs=[row(d), row(B_INNER), row(d, OFF_GA), row(d, OFF_GB), row(d),
                  pl.BlockSpec((1, 6, d), lambda i: (i // tiles_per_batch, 0, 0)),
                  full((d, d)), full((B_INNER, d)), full((d, d)),
                  full((1, d)), full((1, d))],
        out_specs=row(d),
        out_shape=jax.ShapeDtypeStruct((n_tok, d), F32),
        compiler_params=_cparams(("parallel",)),
        name="merge",
    )(ya2, yb2, proj2, proj2, x2, mod3, w_a, w_b, w_o, ln_g.reshape(1, d), ln_b.reshape(1, d))


FFN_ROWS = 256


def _ffn_kernel(x_ref, mod_ref, wg_ref, wu_ref, wd_ref, lg_ref, lbias_ref, o_ref):
    m = mod_ref[0]
    for r in range(0, x_ref.shape[0], FFN_ROWS):
        rs = slice(r, r + FFN_ROWS)
        x = x_ref[rs, :]
        u = (_layer_norm(x) * (1.0 + m[4:5, :]) + m[3:4, :]).astype(BF16)
        hidden = _silu(_dot(u, wg_ref[...])) * _dot(u, wu_ref[...])
        h = _dot(hidden.astype(BF16), wd_ref[...])
        y = _layer_norm(ALPHA * x + m[5:6, :] * h)
        o_ref[rs, :] = y * lg_ref[...] + lbias_ref[...]


def _ffn(x2, mod3, w_gate, w_up, w_down, ln_g, ln_b, seq):
    n_tok, d = x2.shape
    tm = 1024
    tiles_per_batch = seq // tm
    once = lambda shape: pl.BlockSpec(shape, lambda i: (0, 0), pipeline_mode=pl.Buffered(1))
    return pl.pallas_call(
        _ffn_kernel,
        grid=(n_tok // tm,),
        in_specs=[pl.BlockSpec((tm, d), lambda i: (i, 0)),
                  pl.BlockSpec((1, 6, d), lambda i: (i // tiles_per_batch, 0, 0)),
                  once((d, D_FF)), once((d, D_FF)), once((D_FF, d)),
                  once((1, d)), once((1, d))],
        out_specs=pl.BlockSpec((tm, d), lambda i: (i, 0)),
        out_shape=jax.ShapeDtypeStruct((n_tok, d), F32),
        compiler_params=_cparams(("parallel",)),
        name="ffn",
    )(x2, mod3, w_gate, w_up, w_down, ln_g.reshape(1, d), ln_b.reshape(1, d))


def kernel(x, c, w_ada, b_ada, w_in, hgrn_lb, hgrn_gnorm, ssm_conv_w, ssm_conv_b, ssm_dt_bias,
           ssm_a_log, ssm_d, ssm_norm, w_branch_a, w_branch_b, w_o, ln1_g, ln1_b,
           w_ffn_gate, w_ffn_up, w_ffn_down, ln2_g, ln2_b):
    bsz, seq, d = x.shape
    assert d == D_MODEL and w_ada.shape[0] == DEPTH == 1
    x2 = x.reshape(bsz * seq, d)
    for l in range(DEPTH):
        mod3 = _ada(c, w_ada[l], b_ada[l]).reshape(bsz, 6, d)
        w = w_in[l]
        dt_lo = OFF_X + B_INNER + 2 * B_GROUPS * B_STATE
        w_main = jnp.concatenate([w[:, :dt_lo], w[:, dt_lo + B_HEADS:]], axis=1).astype(BF16)
        w_dt = jnp.pad(w[:, dt_lo:dt_lo + B_HEADS], ((0, 0), (0, DT_PAD - B_HEADS))).astype(BF16)
        proj2, lg2, dt2 = _in_proj(x2, mod3, w_main, w_dt, hgrn_lb, ssm_conv_w[l], ssm_conv_b[l], seq)
        proj3 = proj2.reshape(bsz, seq, MAIN_DIM)
        ya = _hgrn(proj3, lg2.reshape(bsz, seq, D_MODEL), hgrn_gnorm[l])
        yb = _ssd(proj3, dt2.reshape(bsz, seq, DT_PAD),
                  ssm_dt_bias[l], ssm_a_log[l], ssm_d[l], ssm_norm[l])
        x2 = _merge(ya.reshape(bsz * seq, d), yb.reshape(bsz * seq, B_INNER), proj2, x2, mod3,
                    w_branch_a[l].astype(BF16), w_branch_b[l].astype(BF16), w_o[l].astype(BF16),
                    ln1_g[l], ln1_b[l], seq)
        x2 = _ffn(x2, mod3, w_ffn_gate[l].astype(BF16), w_ffn_up[l].astype(BF16),
                  w_ffn_down[l].astype(BF16), ln2_g[l], ln2_b[l], seq)
    return x2.reshape(bsz, seq, d)
```
